```python
import jax, jax.numpy as jnp
from jax import lax
import numpy as np

D_MODEL = 1024
BATCH = 16
SEQ = 4096
DEPTH = 4

N_MIXERS = 4
D_FF = 2816
RMS_EPS = 1e-6
LN_EPS = 1e-5
CONV_WIDTH = 31
FOX_HEADS = 16
FOX_HEAD_DIM = D_MODEL // FOX_HEADS
FOX_BLOCK = 128
HGRN_EXPAND = 128
HGRN_HEADS = D_MODEL // HGRN_EXPAND
HGRN_DK = HGRN_HEADS * HGRN_EXPAND
HGRN_DV = D_MODEL
HGRN_HEAD_DV = HGRN_DV // HGRN_HEADS
HGRN_CHUNK = 32
POOL_WINDOWS = (2, 4, 8, 16)
POOL_GROUP = D_MODEL // len(POOL_WINDOWS)

kernel_name = "hybrid_conv_fox_hgrn2_pool_macaron"


def rms_norm(x, g):
    x32 = x.astype(jnp.float32)
    y = x32 * lax.rsqrt(jnp.mean(x32 * x32, axis=-1, keepdims=True) + RMS_EPS)
    return (y * g.astype(jnp.float32)).astype(x.dtype)


def layer_norm(x, g, b):
    x32 = x.astype(jnp.float32)
    mu = jnp.mean(x32, axis=-1, keepdims=True)
    xc = x32 - mu
    y = xc * lax.rsqrt(jnp.mean(xc * xc, axis=-1, keepdims=True) + LN_EPS)
    return (y * g.astype(jnp.float32) + b.astype(jnp.float32)).astype(x.dtype)


def swiglu(h, w_gate, w_up, w_down):
    return (jax.nn.silu(h @ w_gate) * (h @ w_up)) @ w_down


def conv_module(h, w_in, b_in, dw, dw_b, ln_g, ln_b, w_out):
    a, b = jnp.split(h @ w_in + b_in, 2, axis=-1)
    u = a * jax.nn.sigmoid(b)
    u = lax.conv_general_dilated(
        u, dw[:, None, :], window_strides=(1,),
        padding=((CONV_WIDTH - 1, 0),),
        dimension_numbers=('NWC', 'WIO', 'NWC'),
        feature_group_count=D_MODEL) + dw_b
    u = jax.nn.silu(layer_norm(u, ln_g, ln_b))
    return u @ w_out


def fox_attention(h, w_in, b_f, w_out):
    B, S, _ = h.shape
    proj = h @ w_in
    def heads(t):
        return t.reshape(B, S, FOX_HEADS, FOX_HEAD_DIM).transpose(0, 2, 1, 3)
    q = heads(proj[..., :D_MODEL])
    k = heads(proj[..., D_MODEL:2 * D_MODEL])
    v = heads(proj[..., 2 * D_MODEL:3 * D_MODEL])
    log_f = jax.nn.log_sigmoid((proj[..., 3 * D_MODEL:] + b_f).astype(jnp.float32))
    c = jnp.cumsum(log_f, axis=1).transpose(0, 2, 1)
    scale = FOX_HEAD_DIM ** -0.5
    outs = []
    for blk in range(S // FOX_BLOCK):
        q0, q1 = blk * FOX_BLOCK, (blk + 1) * FOX_BLOCK
        logits = (jnp.einsum('bhqd,bhkd->bhqk', q[:, :, q0:q1], k[:, :, :q1]).astype(jnp.float32) * scale
                  + c[:, :, q0:q1, None] - c[:, :, None, :q1])
        causal = (q0 + jnp.arange(FOX_BLOCK))[:, None] >= jnp.arange(q1)[None, :]
        p = jax.nn.softmax(jnp.where(causal, logits, -jnp.inf), axis=-1)
        outs.append(jnp.einsum('bhqk,bhkd->bhqd', p.astype(v.dtype), v[:, :, :q1]))
    o = jnp.concatenate(outs, axis=2).transpose(0, 2, 1, 3).reshape(B, S, D_MODEL)
    return o @ w_out


def hgrn2_mixer(h, w_in, lb, norm_g, w_out):
    B, S, _ = h.shape
    proj = h @ w_in
    q = jax.nn.silu(proj[..., :HGRN_DK]).astype(jnp.float32)
    f_raw = proj[..., HGRN_DK:2 * HGRN_DK].astype(jnp.float32)
    i_in = proj[..., 2 * HGRN_DK:2 * HGRN_DK + HGRN_DV].astype(jnp.float32)
    g_out = proj[..., 2 * HGRN_DK + HGRN_DV:]
    lb = lb.astype(jnp.float32)
    log_f = jnp.logaddexp(jnp.log(lb), jnp.log1p(-lb) + jax.nn.log_sigmoid(f_raw))
    k = (1.0 - lb) * jax.nn.sigmoid(-f_raw)
    n_chunks = S // HGRN_CHUNK

    def to_chunks(t, d):
        return t.reshape(B, n_chunks, HGRN_CHUNK, HGRN_HEADS, d).transpose(1, 0, 3, 2, 4)

    qc = to_chunks(q, HGRN_EXPAND)
    kc = to_chunks(k, HGRN_EXPAND)
    gc = to_chunks(log_f, HGRN_EXPAND)
    vc = to_chunks(i_in, HGRN_HEAD_DV)
    causal = jnp.tril(jnp.ones((HGRN_CHUNK, HGRN_CHUNK), dtype=bool))[:, :, None]

    def step(state, inp):
        q_t, k_t, g_t, v_t = inp
        G = jnp.cumsum(g_t, axis=2)
        o_inter = jnp.einsum('bhtk,bhkv->bhtv', q_t * jnp.exp(G), state)
        diff = G[:, :, :, None, :] - G[:, :, None, :, :]
        decay = jnp.exp(jnp.where(causal, diff, -jnp.inf))
        A = jnp.einsum('bhtk,bhtsk,bhsk->bhts', q_t, decay, k_t)
        o = o_inter + jnp.einsum('bhts,bhsv->bhtv', A, v_t)
        G_last = G[:, :, -1:, :]
        state = (jnp.exp(G_last[:, :, 0, :])[..., None] * state
                 + jnp.einsum('bhsk,bhsv->bhkv', k_t * jnp.exp(G_last - G), v_t))
        return state, o

    state0 = jnp.zeros((B, HGRN_HEADS, HGRN_EXPAND, HGRN_HEAD_DV), jnp.float32)
    _, o = lax.scan(step, state0, (qc, kc, gc, vc))
    o = o.transpose(1, 0, 3, 2, 4).reshape(B, S, HGRN_HEADS, HGRN_HEAD_DV)
    o = o * lax.rsqrt(jnp.mean(o * o, axis=-1, keepdims=True) + RMS_EPS)
    o = o * norm_g.astype(jnp.float32).reshape(HGRN_HEADS, HGRN_HEAD_DV)
    o = o.reshape(B, S, HGRN_DV) * jax.nn.silu(g_out.astype(jnp.float32))
    return o.astype(h.dtype) @ w_out


def pool_mixer(h, w, scale):
    B, S, _ = h.shape
    h32 = h.astype(jnp.float32)
    pos = jnp.arange(1, S + 1, dtype=jnp.float32)[None, :, None]
    outs = []
    for gi, win in enumerate(POOL_WINDOWS):
        xg = h32[..., gi * POOL_GROUP:(gi + 1) * POOL_GROUP]
        cs = jnp.cumsum(xg, axis=1)
        lag = jnp.pad(cs, ((0, 0), (win, 0), (0, 0)))[:, :S]
        mean = (cs - lag) / jnp.minimum(pos, float(win))
        outs.append((mean - xg).astype(h.dtype) @ w[gi])
    return jnp.concatenate(outs, axis=-1) * scale


def setup_inputs(seed: int = 0) -> dict:
    key = jax.random.key(seed)
    ks = jax.random.split(key, 32)
    n_a, n_b, n_c, n_d = (len(range(m, DEPTH, N_MIXERS)) for m in range(N_MIXERS))
    f32 = jnp.float32

    def w(k, shape, fan_in):
        return jax.random.normal(k, shape, f32) * fan_in ** -0.5

    def gain(k, shape):
        return 1.0 + 0.02 * jax.random.normal(k, shape, f32)

    def bias(k, shape):
        return 0.02 * jax.random.normal(k, shape, f32)

    return {
        "x": jax.random.normal(ks[0], (BATCH, SEQ, D_MODEL), f32),
        "ffn_norm": gain(ks[1], (DEPTH, 2, D_MODEL)),
        "ffn_w_gate": w(ks[2], (DEPTH, 2, D_MODEL, D_FF), D_MODEL),
        "ffn_w_up": w(ks[3], (DEPTH, 2, D_MODEL, D_FF), D_MODEL),
        "ffn_w_down": w(ks[4], (DEPTH, 2, D_FF, D_MODEL), D_FF),
        "mix_norm": gain(ks[5], (DEPTH, D_MODEL)),
        "final_norm": gain(ks[6], (D_MODEL,)),
        "conv_w_in": w(ks[7], (n_a, D_MODEL, 2 * D_MODEL), D_MODEL),
        "conv_b_in": bias(ks[8], (n_a, 2 * D_MODEL)),
        "conv_dw": w(ks[9], (n_a, CONV_WIDTH, D_MODEL), CONV_WIDTH),
        "conv_dw_b": bias(ks[10], (n_a, D_MODEL)),
        "conv_ln_g": gain(ks[11], (n_a, D_MODEL)),
        "conv_ln_b": bias(ks[12], (n_a, D_MODEL)),
        "conv_w_out": w(ks[13], (n_a, D_MODEL, D_MODEL), D_MODEL),
        "fox_w_in": w(ks[14], (n_b, D_MODEL, 3 * D_MODEL + FOX_HEADS), D_MODEL),
        "fox_b_f": 1.0 + 3.0 * jax.random.uniform(ks[15], (n_b, FOX_HEADS), f32),
        "fox_w_out": w(ks[16], (n_b, D_MODEL, D_MODEL), D_MODEL),
        "hgrn_w_in": w(ks[17], (n_c, D_MODEL, 2 * HGRN_DK + 2 * HGRN_DV), D_MODEL),
        "hgrn_lb_logits": 0.1 * jax.random.normal(ks[18], (DEPTH, HGRN_DK), f32),
        "hgrn_norm": gain(ks[19], (n_c, HGRN_DV)),
        "hgrn_w_out": w(ks[20], (n_c, HGRN_DV, D_MODEL), HGRN_DV),
        "pool_w": w(ks[21], (n_d, len(POOL_WINDOWS), POOL_GROUP, POOL_GROUP), POOL_GROUP),
        "pool_scale": 1.0 + 0.1 * jax.random.normal(ks[22], (n_d, D_MODEL), f32),
    }


def reference(x, ffn_norm, ffn_w_gate, ffn_w_up, ffn_w_down, mix_norm, final_norm,
              conv_w_in, conv_b_in, conv_dw, conv_dw_b, conv_ln_g, conv_ln_b, conv_w_out,
              fox_w_in, fox_b_f, fox_w_out,
              hgrn_w_in, hgrn_lb_logits, hgrn_norm, hgrn_w_out,
              pool_w, pool_scale):
    p = jax.nn.softmax(hgrn_lb_logits.astype(jnp.float32), axis=0)
    lower_bounds = jnp.cumsum(p, axis=0) - p[0]
    for i in range(DEPTH):
        m, j = i % N_MIXERS, i // N_MIXERS
        x = x + 0.5 * swiglu(rms_norm(x, ffn_norm[i, 0]), ffn_w_gate[i, 0], ffn_w_up[i, 0], ffn_w_down[i, 0])
        h = rms_norm(x, mix_norm[i])
        if m == 0:
            y = conv_module(h, conv_w_in[j], conv_b_in[j], conv_dw[j], conv_dw_b[j],
                            conv_ln_g[j], conv_ln_b[j], conv_w_out[j])
        elif m == 1:
            y = fox_attention(h, fox_w_in[j], fox_b_f[j], fox_w_out[j])
        elif m == 2:
            y = hgrn2_mixer(h, hgrn_w_in[j], lower_bounds[i], hgrn_norm[j], hgrn_w_out[j])
        else:
            y = pool_mixer(h, pool_w[j], pool_scale[j])
        x = x + y
        x = x + 0.5 * swiglu(rms_norm(x, ffn_norm[i, 1]), ffn_w_gate[i, 1], ffn_w_up[i, 1], ffn_w_down[i, 1])
    return rms_norm(x, final_norm)
```

```python
import functools

import jax
import jax.numpy as jnp
from jax import lax
from jax.experimental import pallas as pl
from jax.experimental.pallas import tpu as pltpu

F32 = jnp.float32
BF16 = jnp.bfloat16

RMS_EPS = 1e-6
LN_EPS = 1e-5
CONV_WIDTH = 31
CONV_HALO = 32
CONV_ROWS = 64
FOX_HEADS = 16
FOX_HEAD_DIM = 64
HGRN_HEADS = 8
HGRN_HEAD_DIM = 128
POOL_WINDOWS = (2, 4, 8, 16)
POOL_HALO = 16
LANES = 128
VMEM_LIMIT_BYTES = 56 * 1024 * 1024


def _cparams(n_grid_dims):
    return pltpu.CompilerParams(
        dimension_semantics=("arbitrary",) * n_grid_dims,
        vmem_limit_bytes=VMEM_LIMIT_BYTES)


def _resident(shape):
    zeros = (0,) * len(shape)
    return pl.BlockSpec(shape, lambda *_: zeros, pipeline_mode=pl.Buffered(1))


def _rms(x, g):
    return x * lax.rsqrt(jnp.mean(x * x, axis=-1, keepdims=True) + RMS_EPS) * g


def _silu(x):
    return x * jax.nn.sigmoid(x)


def _log_sigmoid(x):
    return jnp.minimum(x, 0.0) - jnp.log1p(jnp.exp(-jnp.abs(x)))


def _dot(a, b):
    return jnp.dot(a, b, preferred_element_type=F32)


def _dot_nt(a, b):
    return lax.dot_general(a, b, (((1,), (1,)), ((), ())), preferred_element_type=F32)


def _dot_tn(a, b):
    return lax.dot_general(a, b, (((0,), (0,)), ((), ())), preferred_element_type=F32)


def _split_bf16(x, parts):
    out = []
    for _ in range(parts - 1):
        p = x.astype(BF16)
        out.append(p)
        x = x - p.astype(F32)
    out.append(x.astype(BF16))
    return out


def _dot_split(w, parts):
    acc = _dot(w, parts[0])
    for p in parts[1:]:
        acc = acc + _dot(w, p)
    return acc


def _ffn_kernel(x_ref, g_ref, wg_ref, wu_ref, wd_ref, *rest, ff_chunks, final_norm):
    o_ref = rest[-1]
    x = x_ref[...]
    h = _rms(x, g_ref[...]).astype(BF16)
    acc = jnp.zeros_like(x)
    for c0, c1 in ff_chunks:
        gate = _dot(h, wg_ref[:, c0:c1])
        up = _dot(h, wu_ref[:, c0:c1])
        act = (_silu(gate) * up).astype(BF16)
        acc = acc + _dot(act, wd_ref[c0:c1, :])
    y = x + 0.5 * acc
    if final_norm:
        y = _rms(y, rest[0][...])
    o_ref[...] = y


def _ffn_chunks(d_ff, width):
    return tuple((c0, min(c0 + width, d_ff)) for c0 in range(0, d_ff, width))


def _ffn(x2, g, wg, wu, wd, final_g=None, *, tm=512, chunk=768):
    m, d = x2.shape
    d_ff = wg.shape[1]
    final_norm = final_g is not None
    row = pl.BlockSpec((tm, d), lambda i: (i, 0))
    in_specs = [row, _resident((1, d)), _resident((d, d_ff)), _resident((d, d_ff)), _resident((d_ff, d))]
    args = [x2, g.reshape(1, d), wg, wu, wd]
    if final_norm:
        in_specs.append(_resident((1, d)))
        args.append(final_g.reshape(1, d))
    return pl.pallas_call(
        functools.partial(_ffn_kernel, ff_chunks=_ffn_chunks(d_ff, chunk), final_norm=final_norm),
        grid=(m // tm,),
        in_specs=in_specs,
        out_specs=row,
        out_shape=jax.ShapeDtypeStruct((m, d), F32),
        compiler_params=_cparams(1),
        name="ffn",
    )(*args)


def _conv_kernel(x_ref, g_ref, win_ref, bin_ref, dw_ref, dwb_ref, lng_ref, lnb_ref, wout_ref, o_ref,
                 ubuf, cbuf, *, tm):
    d = x_ref.shape[-1]

    @pl.when(pl.program_id(1) == 0)
    def _():
        ubuf[0:CONV_HALO, :] = jnp.zeros((CONV_HALO, d), F32)

    x = x_ref[0]
    h = _rms(x, g_ref[...]).astype(BF16)
    p = _dot(h, win_ref[...]) + bin_ref[...]
    ubuf[CONV_HALO:CONV_HALO + tm, :] = p[:, :d] * jax.nn.sigmoid(p[:, d:])

    base = CONV_HALO - (CONV_WIDTH - 1)
    for r0 in range(0, tm, CONV_ROWS):
        for c0 in range(0, d, LANES):
            cs = slice(c0, c0 + LANES)
            acc = jnp.broadcast_to(dwb_ref[:, cs], (CONV_ROWS, LANES))
            for k in range(CONV_WIDTH):
                lo = r0 + base + k
                acc = acc + dw_ref[k:k + 1, cs] * ubuf[lo:lo + CONV_ROWS, cs]
            cbuf[r0:r0 + CONV_ROWS, cs] = acc
    ubuf[0:CONV_HALO, :] = ubuf[tm:tm + CONV_HALO, :]

    c = cbuf[...]
    xc = c - jnp.mean(c, axis=-1, keepdims=True)
    y = xc * lax.rsqrt(jnp.mean(xc * xc, axis=-1, keepdims=True) + LN_EPS) * lng_ref[...] + lnb_ref[...]
    o_ref[0] = x + _dot(_silu(y).astype(BF16), wout_ref[...])


def _conv_mixer(x, g, w_in, b_in, dw, dw_b, ln_g, ln_b, w_out, *, tm=256):
    b, s, d = x.shape
    dw_pad = jnp.zeros((CONV_HALO, d), F32).at[:CONV_WIDTH].set(dw)
    tile = pl.BlockSpec((1, tm, d), lambda bi, si: (bi, si, 0))
    return pl.pallas_call(
        functools.partial(_conv_kernel, tm=tm),
        grid=(b, s // tm),
        in_specs=[tile, _resident((1, d)), _resident((d, 2 * d)), _resident((1, 2 * d)),
                  _resident((CONV_HALO, d)), _resident((1, d)), _resident((1, d)), _resident((1, d)),
                  _resident((d, d))],
        out_specs=tile,
        out_shape=jax.ShapeDtypeStruct((b, s, d), F32),
        scratch_shapes=[pltpu.VMEM((CONV_HALO + tm, d), F32), pltpu.VMEM((tm, d), F32)],
        compiler_params=_cparams(2),
        name="conv_mixer",
    )(x, g.reshape(1, d), w_in.astype(BF16), b_in.reshape(1, 2 * d), dw_pad, dw_b.reshape(1, d),
      ln_g.reshape(1, d), ln_b.reshape(1, d), w_out.astype(BF16))


def _pool_kernel(x_ref, g_ref, w_ref, sc_ref, o_ref, hbuf, *, tm):
    d = x_ref.shape[-1]
    group = d // len(POOL_WINDOWS)
    si = pl.program_id(1)

    @pl.when(si == 0)
    def _():
        hbuf[0:POOL_HALO, :] = jnp.zeros((POOL_HALO, d), F32)

    x = x_ref[0]
    hbuf[POOL_HALO:POOL_HALO + tm, :] = _rms(x, g_ref[...])
    pos = (si * tm + 1 + lax.broadcasted_iota(jnp.int32, (tm, 1), 0)).astype(F32)
    outs = []
    for gi, win in enumerate(POOL_WINDOWS):
        cs = slice(gi * group, (gi + 1) * group)
        cur = hbuf[POOL_HALO:POOL_HALO + tm, cs]
        tot = cur
        for j in range(1, win):
            tot = tot + hbuf[POOL_HALO - j:POOL_HALO - j + tm, cs]
        diff = tot / jnp.minimum(pos, float(win)) - cur
        outs.append(_dot(diff.astype(BF16), w_ref[gi]))
    hbuf[0:POOL_HALO, :] = hbuf[tm:tm + POOL_HALO, :]
    o_ref[0] = x + jnp.concatenate(outs, axis=-1) * sc_ref[...]


def _pool_mixer(x, g, w, scale, *, tm=512):
    b, s, d = x.shape
    tile = pl.BlockSpec((1, tm, d), lambda bi, si: (bi, si, 0))
    return pl.pallas_call(
        functools.partial(_pool_kernel, tm=tm),
        grid=(b, s // tm),
        in_specs=[tile, _resident((1, d)), _resident(w.shape), _resident((1, d))],
        out_specs=tile,
        out_shape=jax.ShapeDtypeStruct((b, s, d), F32),
        scratch_shapes=[pltpu.VMEM((POOL_HALO + tm, d), F32)],
        compiler_params=_cparams(2),
        name="pool_mixer",
    )(x, g.reshape(1, d), w.astype(BF16), scale.reshape(1, d))


def _fox_proj_kernel(x_ref, g_ref, wqkv_ref, wf_ref, bf_ref, q_ref, k_ref, v_ref, c_ref, carry, *, tm):
    d = x_ref.shape[-1]

    @pl.when(pl.program_id(1) == 0)
    def _():
        carry[...] = jnp.zeros_like(carry)

    h = _rms(x_ref[0], g_ref[...]).astype(BF16)
    p = _dot(h, wqkv_ref[...])
    scale = FOX_HEAD_DIM ** -0.5
    for j in range(d // LANES):
        cs = slice(j * LANES, (j + 1) * LANES)
        q_ref[0, j] = (p[:, cs] * scale).astype(BF16)
        k_ref[0, j] = p[:, d + j * LANES:d + (j + 1) * LANES].astype(BF16)
        v_ref[0, j] = p[:, 2 * d + j * LANES:2 * d + (j + 1) * LANES].astype(BF16)

    log_f = _log_sigmoid(_dot_nt(wf_ref[...], h) + bf_ref[...])
    src = lax.broadcasted_iota(jnp.int32, (tm, tm), 0)
    dst = lax.broadcasted_iota(jnp.int32, (tm, tm), 1)
    upper = jnp.where(src <= dst, 1.0, 0.0).astype(BF16)
    cum = carry[:, 0:1]
    for part in _split_bf16(log_f, 3):
        cum = cum + _dot(part, upper)
    c_ref[0] = cum
    carry[...] = jnp.broadcast_to(cum[:, tm - 1:tm], carry.shape)


def _fox_flash_kernel(q_ref, k_ref, v_ref, c_ref, o_ref, *, t):
    qi = pl.program_id(2)
    q2 = q_ref[0, 0]
    lane = lax.broadcasted_iota(jnp.int32, (1, LANES), 1)
    row = lax.broadcasted_iota(jnp.int32, (t, t), 0)
    col = lax.broadcasted_iota(jnp.int32, (t, t), 1)
    q_start = pl.multiple_of(qi * t, t)
    outs = []
    for a in range(2):
        head_lanes = (lane < FOX_HEAD_DIM) if a == 0 else (lane >= FOX_HEAD_DIM)
        qa = jnp.where(head_lanes, q2, jnp.zeros_like(q2))
        cq_row = c_ref[0, 0, a:a + 1, pl.ds(q_start, t)]
        cq = jnp.sum(jnp.where(row == col, cq_row, 0.0), axis=1, keepdims=True)

        def step(kj, carry, *, masked, qa=qa, cq=cq, a=a):
            m, l, acc = carry
            k_start = pl.multiple_of(kj * t, t)
            k2 = k_ref[0, 0, pl.ds(k_start, t), :]
            v2 = v_ref[0, 0, pl.ds(k_start, t), :]
            s = _dot_nt(qa, k2) + cq - c_ref[0, 0, a:a + 1, pl.ds(k_start, t)]
            if masked:
                s = jnp.where(row >= col, s, -jnp.inf)
            m_new = jnp.maximum(m, jnp.max(s, axis=1, keepdims=True))
            alpha = jnp.exp(m - m_new)
            p = jnp.exp(s - m_new)
            l = alpha * l + jnp.sum(p, axis=1, keepdims=True)
            acc = alpha * acc + _dot(p.astype(BF16), v2)
            return m_new, l, acc

        init = (jnp.full((t, 1), -jnp.inf, F32), jnp.zeros((t, 1), F32), jnp.zeros((t, LANES), F32))
        carry = lax.fori_loop(0, qi, functools.partial(step, masked=False), init)
        _, l, acc = step(qi, carry, masked=True)
        outs.append(acc / l)
    o_ref[0, 0] = jnp.where(lane < FOX_HEAD_DIM, outs[0], outs[1]).astype(BF16)


def _fox_out_kernel(x_ref, o_ref, w_ref, y_ref):
    o = jnp.concatenate([o_ref[0, j] for j in range(o_ref.shape[1])], axis=-1)
    y_ref[0] = x_ref[0] + _dot(o, w_ref[...])


def _fox_mixer(x, g, w_in, b_f, w_out, *, tm=512, t=512):
    b, s, d = x.shape
    pairs = d // LANES
    w_qkv = w_in[:, :3 * d].astype(BF16)
    w_f = w_in[:, 3 * d:].T.astype(BF16)
    tile = pl.BlockSpec((1, tm, d), lambda bi, si: (bi, si, 0))
    heads_tile = pl.BlockSpec((1, pairs, tm, LANES), lambda bi, si: (bi, 0, si, 0))
    qkv_shape = jax.ShapeDtypeStruct((b, pairs, s, LANES), BF16)
    q, k, v, c = pl.pallas_call(
        functools.partial(_fox_proj_kernel, tm=tm),
        grid=(b, s // tm),
        in_specs=[tile, _resident((1, d)), _resident((d, 3 * d)), _resident((FOX_HEADS, d)),
                  _resident((FOX_HEADS, 1))],
        out_specs=[heads_tile, heads_tile, heads_tile,
                   pl.BlockSpec((1, FOX_HEADS, tm), lambda bi, si: (bi, 0, si))],
        out_shape=[qkv_shape, qkv_shape, qkv_shape, jax.ShapeDtypeStruct((b, FOX_HEADS, s), F32)],
        scratch_shapes=[pltpu.VMEM((FOX_HEADS, LANES), F32)],
        compiler_params=_cparams(2),
        name="fox_proj",
    )(x, g.reshape(1, d), w_qkv, w_f, b_f.reshape(FOX_HEADS, 1))

    q_tile = pl.BlockSpec((1, 1, t, LANES), lambda bi, pi, qi: (bi, pi, qi, 0))
    kv_all = pl.BlockSpec((1, 1, s, LANES), lambda bi, pi, qi: (bi, pi, 0, 0))
    o = pl.pallas_call(
        functools.partial(_fox_flash_kernel, t=t),
        grid=(b, pairs, s // t),
        in_specs=[q_tile, kv_all, kv_all,
                  pl.BlockSpec((1, 1, 2, s), lambda bi, pi, qi: (bi, pi, 0, 0))],
        out_specs=q_tile,
        out_shape=qkv_shape,
        compiler_params=_cparams(3),
        name="fox_flash",
    )(q, k, v, c.reshape(b, pairs, 2, s))

    return pl.pallas_call(
        _fox_out_kernel,
        grid=(b, s // tm),
        in_specs=[tile, heads_tile, _resident((d, d))],
        out_specs=tile,
        out_shape=jax.ShapeDtypeStruct((b, s, d), F32),
        compiler_params=_cparams(2),
        name="fox_out",
    )(x, o, w_out.astype(BF16))


def _hgrn_kernel(x_ref, g_ref, win_ref, lbl_ref, ng_ref, wout_ref, o_ref, state, *, c, layer):
    d = x_ref.shape[-1]
    dk = HGRN_HEADS * HGRN_HEAD_DIM

    @pl.when(pl.program_id(1) == 0)
    def _():
        state[...] = jnp.zeros_like(state)

    x = x_ref[0]
    h = _rms(x, g_ref[...]).astype(BF16)
    proj = _dot(h, win_ref[...])
    q = _silu(proj[:, :dk])
    f_raw = proj[:, dk:2 * dk]
    v = proj[:, 2 * dk:2 * dk + d].astype(BF16)
    g_out = proj[:, 2 * dk + d:]

    logits = lbl_ref[...]
    e = jnp.exp(logits - jnp.max(logits, axis=0, keepdims=True))
    p_layers = e / jnp.sum(e, axis=0, keepdims=True)
    if layer > 0:
        lb = jnp.sum(p_layers[1:layer + 1], axis=0, keepdims=True)
    else:
        lb = jnp.zeros_like(logits[0:1])

    a_ = jnp.log(lb)
    b_ = jnp.log1p(-lb) + _log_sigmoid(f_raw)
    log_f = jnp.maximum(a_, b_) + jnp.log1p(jnp.exp(-jnp.abs(a_ - b_)))
    kk = (1.0 - lb) * jax.nn.sigmoid(-f_raw)

    t_idx = lax.broadcasted_iota(jnp.int32, (c, c), 0)
    j_idx = lax.broadcasted_iota(jnp.int32, (c, c), 1)
    t_col = lax.broadcasted_iota(jnp.int32, (c, 1), 0)
    parts = _split_bf16(log_f, 3)
    cum = _dot_split(jnp.where(j_idx <= t_idx, 1.0, 0.0).astype(BF16), parts)
    cum_last = cum[c - 1:c, :]
    q_in = (q * jnp.exp(cum)).astype(BF16)
    k_out = (kk * jnp.exp(cum_last - cum)).astype(BF16)
    decay = jnp.exp(cum_last)

    def head(arr, hd):
        return arr[:, hd * HGRN_HEAD_DIM:(hd + 1) * HGRN_HEAD_DIM]

    q_bf = q.astype(BF16)
    k_bf = kk.astype(BF16)
    pair = [jnp.where(t_idx == j_idx, _dot_nt(head(q_bf, hd), head(k_bf, hd)), 0.0)
            for hd in range(HGRN_HEADS)]
    blk = 2
    while blk <= c:
        half = blk // 2
        mid = (t_idx // blk) * blk + (half - 1)
        span = (j_idx > jnp.minimum(t_idx, mid)) & (j_idx <= jnp.maximum(t_idx, mid))
        ex = jnp.exp(_dot_split(jnp.where(span, 1.0, 0.0).astype(BF16), parts[:2]))
        later_col = (t_col % blk) >= half
        q_l = jnp.where(later_col, q * ex, 0.0).astype(BF16)
        k_l = jnp.where(later_col, 0.0, kk * ex).astype(BF16)
        same_block = (t_idx // blk) == (j_idx // blk)
        for hd in range(HGRN_HEADS):
            pair[hd] = pair[hd] + jnp.where(same_block, _dot_nt(head(q_l, hd), head(k_l, hd)), 0.0)
        blk *= 2

    outs = []
    for hd in range(HGRN_HEADS):
        st = state[hd]
        v_h = head(v, hd)
        o_h = _dot_nt(head(q_in, hd), st.astype(BF16)) + _dot(pair[hd].astype(BF16), v_h)
        state[hd] = head(decay, hd) * st + _dot_tn(v_h, head(k_out, hd))
        outs.append(o_h * lax.rsqrt(jnp.mean(o_h * o_h, axis=-1, keepdims=True) + RMS_EPS))
    o = jnp.concatenate(outs, axis=-1) * ng_ref[...] * _silu(g_out)
    o_ref[0] = x + _dot(o.astype(BF16), wout_ref[...])


def _hgrn_mixer(x, g, w_in, lb_logits, norm_g, w_out, *, layer, c=256):
    b, s, d = x.shape
    tile = pl.BlockSpec((1, c, d), lambda bi, si: (bi, si, 0))
    return pl.pallas_call(
        functools.partial(_hgrn_kernel, c=c, layer=layer),
        grid=(b, s // c),
        in_specs=[tile, _resident((1, d)), _resident(w_in.shape), _resident(lb_logits.shape),
                  _resident((1, d)), _resident((d, d))],
        out_specs=tile,
        out_shape=jax.ShapeDtypeStruct((b, s, d), F32),
        scratch_shapes=[pltpu.VMEM((HGRN_HEADS, HGRN_HEAD_DIM, HGRN_HEAD_DIM), F32)],
        compiler_params=_cparams(2),
        name="hgrn_mixer",
    )(x, g.reshape(1, d), w_in.astype(BF16), lb_logits, norm_g.reshape(1, d), w_out.astype(BF16))


def kernel(x, ffn_norm, ffn_w_gate, ffn_w_up, ffn_w_down, mix_norm, final_norm, conv_w_in, conv_b_in, conv_dw, conv_dw_b, conv_ln_g, conv_ln_b, conv_w_out, fox_w_in, fox_b_f, fox_w_out, hgrn_w_in, hgrn_lb_logits, hgrn_norm, hgrn_w_out, pool_w, pool_scale):
    b, s, d = x.shape
    depth = ffn_norm.shape[0]
    n_mixers = 4
    wg = ffn_w_gate.astype(BF16)
    wu = ffn_w_up.astype(BF16)
    wd = ffn_w_down.astype(BF16)

    def ffn(xx, i, k, final_g=None):
        return _ffn(xx.reshape(b * s, d), ffn_norm[i, k], wg[i, k], wu[i, k], wd[i, k], final_g).reshape(b, s, d)

    for i in range(depth):
        m, j = i % n_mixers, i // n_mixers
        x = ffn(x, i, 0)
        if m == 0:
            x = _conv_mixer(x, mix_norm[i], conv_w_in[j], conv_b_in[j], conv_dw[j], conv_dw_b[j],
                            conv_ln_g[j], conv_ln_b[j], conv_w_out[j])
        elif m == 1:
            x = _fox_mixer(x, mix_norm[i], fox_w_in[j], fox_b_f[j], fox_w_out[j])
        elif m == 2:
            x = _hgrn_mixer(x, mix_norm[i], hgrn_w_in[j], hgrn_lb_logits, hgrn_norm[j], hgrn_w_out[j], layer=i)
        else:
            x = _pool_mixer(x, mix_norm[i], pool_w[j], pool_scale[j])
        x = ffn(x, i, 1, final_norm if i == depth - 1 else None)
    return x
```

```python
import functools

import jax
import jax.numpy as jnp
import numpy as np
from jax import lax
from jax.experimental import pallas as pl
from jax.experimental.pallas import tpu as pltpu

F32 = jnp.float32
BF16 = jnp.bfloat16

RMS_EPS = 1e-6
LN_EPS = 1e-5
LOG2_E = 1.4426950408889634
CONV_WIDTH = 31
CONV_HALO = 32
CONV_ROWS = 128
FOX_HEADS = 16
FOX_HEAD_DIM = 64
FOX_BIAS_TERMS = 3
FOX_VT_ROWS = 80
HGRN_HEADS = 8
HGRN_HEAD_DIM = 128
POOL_WINDOWS = (2, 4, 8, 16)
POOL_HALO = 16
LANES = 128
SUBLANES = 8
VMEM_LIMIT_BYTES = 56 * 1024 * 1024


def _cparams(n_grid_dims):
    return pltpu.CompilerParams(
        dimension_semantics=("arbitrary",) * n_grid_dims,
        vmem_limit_bytes=VMEM_LIMIT_BYTES)


def _resident(shape):
    zeros = (0,) * len(shape)
    return pl.BlockSpec(shape, lambda *_: zeros, pipeline_mode=pl.Buffered(1))


def _rms(x, g):
    return x * lax.rsqrt(jnp.mean(x * x, axis=-1, keepdims=True) + RMS_EPS) * g


def _silu(x):
    return x * jax.nn.sigmoid(x)


def _log_sigmoid(x):
    return jnp.minimum(x, 0.0) - jnp.log1p(jnp.exp(-jnp.abs(x)))


def _dot(a, b):
    return jnp.dot(a, b, preferred_element_type=F32)


def _dot_nt(a, b):
    return lax.dot_general(a, b, (((1,), (1,)), ((), ())), preferred_element_type=F32)


def _dot_tn(a, b):
    return lax.dot_general(a, b, (((0,), (0,)), ((), ())), preferred_element_type=F32)


def _split_bf16(x, parts):
    out = []
    for _ in range(parts - 1):
        p = x.astype(BF16)
        out.append(p)
        x = x - p.astype(F32)
    out.append(x.astype(BF16))
    return out


def _dot_split(w, parts):
    acc = _dot(w, parts[0])
    for p in parts[1:]:
        acc = acc + _dot(w, p)
    return acc


def _ffn_kernel(x_ref, g_ref, wg_ref, wu_ref, wd_ref, *rest, ff_chunks, final_norm):
    o_ref = rest[-1]
    x = x_ref[...]
    h = _rms(x, g_ref[...]).astype(BF16)
    acc = jnp.zeros_like(x)
    for c0, c1 in ff_chunks:
        gate = _dot(h, wg_ref[:, c0:c1])
        up = _dot(h, wu_ref[:, c0:c1])
        act = (_silu(gate) * up).astype(BF16)
        acc = acc + _dot(act, wd_ref[c0:c1, :])
    y = x + 0.5 * acc
    if final_norm:
        y = _rms(y, rest[0][...])
    o_ref[...] = y


def _ffn_chunks(d_ff, width):
    return tuple((c0, min(c0 + width, d_ff)) for c0 in range(0, d_ff, width))


def _ffn(x2, g, wg, wu, wd, final_g=None, *, tm=512, chunk=768):
    m, d = x2.shape
    d_ff = wg.shape[1]
    final_norm = final_g is not None
    row = pl.BlockSpec((tm, d), lambda i: (i, 0))
    in_specs = [row, _resident((1, d)), _resident((d, d_ff)), _resident((d, d_ff)), _resident((d_ff, d))]
    args = [x2, g.reshape(1, d), wg, wu, wd]
    if final_norm:
        in_specs.append(_resident((1, d)))
        args.append(final_g.reshape(1, d))
    return pl.pallas_call(
        functools.partial(_ffn_kernel, ff_chunks=_ffn_chunks(d_ff, chunk), final_norm=final_norm),
        grid=(m // tm,),
        in_specs=in_specs,
        out_specs=row,
        out_shape=jax.ShapeDtypeStruct((m, d), F32),
        compiler_params=_cparams(1),
        name="ffn",
    )(*args)


def _conv_kernel(x_ref, g_ref, win_ref, bin_ref, dw_ref, dwb_ref, lng_ref, lnb_ref, wout_ref, o_ref,
                 ubuf, cbuf, *, tm):
    d = x_ref.shape[-1]

    @pl.when(pl.program_id(1) == 0)
    def _():
        ubuf[0:CONV_HALO, :] = jnp.zeros((CONV_HALO, d), F32)

    x = x_ref[0]
    h = _rms(x, g_ref[...]).astype(BF16)
    p = _dot(h, win_ref[...]) + bin_ref[...]
    ubuf[CONV_HALO:CONV_HALO + tm, :] = p[:, :d] * jax.nn.sigmoid(p[:, d:])

    base = CONV_HALO - (CONV_WIDTH - 1)
    n_rows = CONV_ROWS + CONV_HALO
    for r0 in range(0, tm, CONV_ROWS):
        for c0 in range(0, d, LANES):
            cs = slice(c0, c0 + LANES)
            col = ubuf[r0:r0 + n_rows, cs]
            acc = jnp.broadcast_to(dwb_ref[:, cs], (CONV_ROWS, LANES))
            for r in range(SUBLANES):
                taps = [k for k in range(CONV_WIDTH) if (base + k) % SUBLANES == r]
                if not taps:
                    continue
                rot = col if r == 0 else pltpu.roll(col, n_rows - r, 0)
                for k in taps:
                    a0 = base + k - r
                    acc = acc + dw_ref[k:k + 1, cs] * rot[a0:a0 + CONV_ROWS]
            cbuf[r0:r0 + CONV_ROWS, cs] = acc
    ubuf[0:CONV_HALO, :] = ubuf[tm:tm + CONV_HALO, :]

    c = cbuf[...]
    xc = c - jnp.mean(c, axis=-1, keepdims=True)
    y = xc * lax.rsqrt(jnp.mean(xc * xc, axis=-1, keepdims=True) + LN_EPS) * lng_ref[...] + lnb_ref[...]
    o_ref[0] = x + _dot(_silu(y).astype(BF16), wout_ref[...])


def _conv_mixer(x, g, w_in, b_in, dw, dw_b, ln_g, ln_b, w_out, *, tm=256):
    b, s, d = x.shape
    dw_pad = jnp.zeros((CONV_HALO, d), F32).at[:CONV_WIDTH].set(dw)
    tile = pl.BlockSpec((1, tm, d), lambda bi, si: (bi, si, 0))
    return pl.pallas_call(
        functools.partial(_conv_kernel, tm=tm),
        grid=(b, s // tm),
        in_specs=[tile, _resident((1, d)), _resident((d, 2 * d)), _resident((1, 2 * d)),
                  _resident((CONV_HALO, d)), _resident((1, d)), _resident((1, d)), _resident((1, d)),
                  _resident((d, d))],
        out_specs=tile,
        out_shape=jax.ShapeDtypeStruct((b, s, d), F32),
        scratch_shapes=[pltpu.VMEM((CONV_HALO + tm, d), F32), pltpu.VMEM((tm, d), F32)],
        compiler_params=_cparams(2),
        name="conv_mixer",
    )(x, g.reshape(1, d), w_in.astype(BF16), b_in.reshape(1, 2 * d), dw_pad, dw_b.reshape(1, d),
      ln_g.reshape(1, d), ln_b.reshape(1, d), w_out.astype(BF16))


def _pool_kernel(x_ref, g_ref, w_ref, sc_ref, o_ref, hbuf, *, tm):
    d = x_ref.shape[-1]
    group = d // len(POOL_WINDOWS)
    si = pl.program_id(1)

    @pl.when(si == 0)
    def _():
        hbuf[0:POOL_HALO, :] = jnp.zeros((POOL_HALO, d), F32)

    x = x_ref[0]
    hbuf[POOL_HALO:POOL_HALO + tm, :] = _rms(x, g_ref[...])
    pos = (si * tm + 1 + lax.broadcasted_iota(jnp.int32, (tm, 1), 0)).astype(F32)
    outs = []
    for gi, win in enumerate(POOL_WINDOWS):
        cs = slice(gi * group, (gi + 1) * group)
        cur = hbuf[POOL_HALO:POOL_HALO + tm, cs]
        tot = cur
        for j in range(1, win):
            tot = tot + hbuf[POOL_HALO - j:POOL_HALO - j + tm, cs]
        diff = tot / jnp.minimum(pos, float(win)) - cur
        outs.append(_dot(diff.astype(BF16), w_ref[gi]))
    hbuf[0:POOL_HALO, :] = hbuf[tm:tm + POOL_HALO, :]
    o_ref[0] = x + jnp.concatenate(outs, axis=-1) * sc_ref[...]


def _pool_mixer(x, g, w, scale, *, tm=512):
    b, s, d = x.shape
    tile = pl.BlockSpec((1, tm, d), lambda bi, si: (bi, si, 0))
    return pl.pallas_call(
        functools.partial(_pool_kernel, tm=tm),
        grid=(b, s // tm),
        in_specs=[tile, _resident((1, d)), _resident(w.shape), _resident((1, d))],
        out_specs=tile,
        out_shape=jax.ShapeDtypeStruct((b, s, d), F32),
        scratch_shapes=[pltpu.VMEM((POOL_HALO + tm, d), F32)],
        compiler_params=_cparams(2),
        name="pool_mixer",
    )(x, g.reshape(1, d), w.astype(BF16), scale.reshape(1, d))


def _fox_bias_selector():
    sel = np.zeros((FOX_BIAS_TERMS * LANES, FOX_HEADS * LANES), np.float32)
    for hd in range(FOX_HEADS):
        z = hd * LANES + (FOX_HEAD_DIM if hd % 2 == 0 else 0)
        for part in range(FOX_BIAS_TERMS):
            sel[part * LANES + hd, z + part] = 1.0
            sel[part * LANES + hd, z + FOX_BIAS_TERMS + part] = 1.0
    return sel


def _fox_proj_kernel(x_ref, g_ref, wqk_ref, wvt_ref, wf_ref, bf_ref, sel_ref, q_ref, k_ref, vt_ref, carry, *, tm):
    d = x_ref.shape[-1]

    @pl.when(pl.program_id(1) == 0)
    def _():
        carry[...] = jnp.zeros_like(carry)

    h = _rms(x_ref[0], g_ref[...]).astype(BF16)
    p = _dot(h, wqk_ref[...])
    vt = _dot_nt(wvt_ref[...], h).astype(BF16)
    ones = jnp.ones((FOX_VT_ROWS - FOX_HEAD_DIM, tm), BF16)
    for hd in range(FOX_HEADS):
        vt_ref[0, hd * FOX_VT_ROWS:hd * FOX_VT_ROWS + FOX_HEAD_DIM, :] = vt[hd * FOX_HEAD_DIM:(hd + 1) * FOX_HEAD_DIM]
        vt_ref[0, hd * FOX_VT_ROWS + FOX_HEAD_DIM:(hd + 1) * FOX_VT_ROWS, :] = ones

    log_f = _log_sigmoid(_dot(h, wf_ref[...]) + bf_ref[...])
    row = lax.broadcasted_iota(jnp.int32, (tm, tm), 0)
    col = lax.broadcasted_iota(jnp.int32, (tm, tm), 1)
    lower = jnp.where(col <= row, 1.0, 0.0).astype(BF16)
    c = carry[0:1, :] + _dot_split(lower, _split_bf16(log_f, 3))
    carry[...] = jnp.broadcast_to(c[tm - 1:tm, :], carry.shape)
    terms = _split_bf16(c * LOG2_E, FOX_BIAS_TERMS)
    bias = _dot(jnp.concatenate(terms, axis=-1), sel_ref[...])

    lane = lax.broadcasted_iota(jnp.int32, (1, LANES), 1)
    scale = FOX_HEAD_DIM ** -0.5 * LOG2_E
    for j in range(d // LANES):
        pq = p[:, j * LANES:(j + 1) * LANES] * scale
        pk = p[:, d + j * LANES:d + (j + 1) * LANES]
        for a in range(2):
            hd = 2 * j + a
            data = (lane < FOX_HEAD_DIM) if a == 0 else (lane >= FOX_HEAD_DIM)
            z = FOX_HEAD_DIM if a == 0 else 0
            first = (lane >= z) & (lane < z + FOX_BIAS_TERMS)
            second = (lane >= z + FOX_BIAS_TERMS) & (lane < z + 2 * FOX_BIAS_TERMS)
            bias_h = bias[:, hd * LANES:(hd + 1) * LANES]
            aug_q = jnp.where(first, bias_h, jnp.where(second, -1.0, 0.0))
            aug_k = jnp.where(second, bias_h, jnp.where(first, 1.0, 0.0))
            q_ref[0, hd] = jnp.where(data, pq, aug_q).astype(BF16)
            k_ref[0, hd] = jnp.where(data, pk, aug_k).astype(BF16)


def _fox_flash_kernel(q_ref, k_ref, vt_ref, o_ref, *, tq, tk, group):
    qi = pl.program_id(1)
    key_idx = lax.broadcasted_iota(jnp.int32, (tk, tq), 0)
    qry_idx = lax.broadcasted_iota(jnp.int32, (tk, tq), 1)
    rows = FOX_VT_ROWS
    ratio = tq // tk

    for g0 in range(0, FOX_HEADS, group):
        heads = tuple(range(g0, g0 + group))

        def tile(kj, carry, *, diag, heads=heads):
            k_start = pl.multiple_of(kj * tk, tk)
            scores = [_dot_nt(k_ref[0, hd, pl.ds(k_start, tk), :], q_ref[0, hd]) for hd in heads]
            probs, stats = [], []
            for s, (m, acc) in zip(scores, carry):
                if diag is not None:
                    s = jnp.where(key_idx + diag * tk <= qry_idx, s, -jnp.inf)
                m_new = jnp.maximum(m, jnp.max(s, axis=0, keepdims=True))
                stats.append((m_new, jnp.exp2(m - m_new)))
                probs.append(jnp.exp2(s - m_new).astype(BF16))
            out = []
            for hd, p, (m_new, alpha), (_, acc) in zip(heads, probs, stats, carry):
                v_t = vt_ref[0, hd * rows:(hd + 1) * rows, pl.ds(k_start, tk)]
                out.append((m_new, alpha * acc + _dot(v_t, p)))
            return tuple(out)

        init = tuple((jnp.full((1, tq), -jnp.inf, F32), jnp.zeros((rows, tq), F32)) for _ in heads)
        carry = lax.fori_loop(0, qi * ratio, functools.partial(tile, diag=None), init)
        for r in range(ratio):
            carry = tile(qi * ratio + r, carry, diag=r)
        for hd, (_, acc) in zip(heads, carry):
            o_ref[0, hd * FOX_HEAD_DIM:(hd + 1) * FOX_HEAD_DIM, :] = (
                acc[:FOX_HEAD_DIM] / acc[FOX_HEAD_DIM:FOX_HEAD_DIM + 1]).astype(BF16)


def _fox_out_kernel(x_ref, ot_ref, w_ref, y_ref):
    y_ref[0] = x_ref[0] + _dot_tn(ot_ref[0], w_ref[...])


def _fox_mixer(x, g, w_in, b_f, w_out, *, tm=512, tq=256, tk=256, group=16):
    b, s, d = x.shape
    vt_rows = FOX_HEADS * FOX_VT_ROWS
    w_qk = w_in[:, :2 * d].astype(BF16)
    w_vt = w_in[:, 2 * d:3 * d].T.astype(BF16)
    w_f = jnp.zeros((d, LANES), F32).at[:, :FOX_HEADS].set(w_in[:, 3 * d:]).astype(BF16)
    b_f_row = jnp.zeros((1, LANES), F32).at[0, :FOX_HEADS].set(b_f)
    tile = pl.BlockSpec((1, tm, d), lambda bi, si: (bi, si, 0))
    heads_tile = pl.BlockSpec((1, FOX_HEADS, tm, LANES), lambda bi, si: (bi, 0, si, 0))
    qk_shape = jax.ShapeDtypeStruct((b, FOX_HEADS, s, LANES), BF16)
    q, k, vt = pl.pallas_call(
        functools.partial(_fox_proj_kernel, tm=tm),
        grid=(b, s // tm),
        in_specs=[tile, _resident((1, d)), _resident((d, 2 * d)), _resident((d, d)), _resident((d, LANES)),
                  _resident((1, LANES)), _resident((FOX_BIAS_TERMS * LANES, FOX_HEADS * LANES))],
        out_specs=[heads_tile, heads_tile, pl.BlockSpec((1, vt_rows, tm), lambda bi, si: (bi, 0, si))],
        out_shape=[qk_shape, qk_shape, jax.ShapeDtypeStruct((b, vt_rows, s), BF16)],
        scratch_shapes=[pltpu.VMEM((SUBLANES, LANES), F32)],
        compiler_params=_cparams(2),
        name="fox_proj",
    )(x, g.reshape(1, d), w_qk, w_vt, w_f, b_f_row, jnp.asarray(_fox_bias_selector(), BF16))

    ot = pl.pallas_call(
        functools.partial(_fox_flash_kernel, tq=tq, tk=tk, group=group),
        grid=(b, s // tq),
        in_specs=[pl.BlockSpec((1, FOX_HEADS, tq, LANES), lambda bi, qi: (bi, 0, qi, 0)),
                  pl.BlockSpec((1, FOX_HEADS, s, LANES), lambda bi, qi: (bi, 0, 0, 0),
                               pipeline_mode=pl.Buffered(1)),
                  pl.BlockSpec((1, vt_rows, s), lambda bi, qi: (bi, 0, 0), pipeline_mode=pl.Buffered(1))],
        out_specs=pl.BlockSpec((1, d, tq), lambda bi, qi: (bi, 0, qi)),
        out_shape=jax.ShapeDtypeStruct((b, d, s), BF16),
        compiler_params=_cparams(2),
        name="fox_flash",
    )(q, k, vt)

    return pl.pallas_call(
        _fox_out_kernel,
        grid=(b, s // tm),
        in_specs=[tile, pl.BlockSpec((1, d, tm), lambda bi, si: (bi, 0, si)), _resident((d, d))],
        out_specs=tile,
        out_shape=jax.ShapeDtypeStruct((b, s, d), F32),
        compiler_params=_cparams(2),
        name="fox_out",
    )(x, ot, w_out.astype(BF16))


def _hgrn_kernel(x_ref, g_ref, win_ref, lbl_ref, ng_ref, wout_ref, o_ref, state, *, c, layer):
    d = x_ref.shape[-1]
    dk = HGRN_HEADS * HGRN_HEAD_DIM

    @pl.when(pl.program_id(1) == 0)
    def _():
        state[...] = jnp.zeros_like(state)

    x = x_ref[0]
    h = _rms(x, g_ref[...]).astype(BF16)
    proj = _dot(h, win_ref[...])
    q = _silu(proj[:, :dk])
    f_raw = proj[:, dk:2 * dk]
    v_f32 = proj[:, 2 * dk:2 * dk + d]
    v = v_f32.astype(BF16)
    g_out = proj[:, 2 * dk + d:]

    logits = lbl_ref[...]
    e = jnp.exp(logits - jnp.max(logits, axis=0, keepdims=True))
    p_layers = e / jnp.sum(e, axis=0, keepdims=True)
    if layer > 0:
        lb = jnp.sum(p_layers[1:layer + 1], axis=0, keepdims=True)
    else:
        lb = jnp.zeros_like(logits[0:1])

    a_ = jnp.log(lb)
    b_ = jnp.log1p(-lb) + _log_sigmoid(f_raw)
    log_f = (jnp.maximum(a_, b_) + jnp.log1p(jnp.exp(-jnp.abs(a_ - b_)))) * LOG2_E
    kk = (1.0 - lb) * jax.nn.sigmoid(-f_raw)

    t_idx = lax.broadcasted_iota(jnp.int32, (c, c), 0)
    j_idx = lax.broadcasted_iota(jnp.int32, (c, c), 1)
    t_col = lax.broadcasted_iota(jnp.int32, (c, 1), 0)
    parts = _split_bf16(log_f, 3)
    cum = _dot_split(jnp.where(j_idx <= t_idx, 1.0, 0.0).astype(BF16), parts)
    cum_last = cum[c - 1:c, :]
    q_in = (q * jnp.exp2(cum)).astype(BF16)
    k_out = (kk * jnp.exp2(cum_last - cum)).astype(BF16)
    decay = jnp.exp2(cum_last)

    def head(arr, hd):
        return arr[:, hd * HGRN_HEAD_DIM:(hd + 1) * HGRN_HEAD_DIM]

    pair = [None] * HGRN_HEADS
    blk = 2
    while blk <= c:
        half = blk // 2
        later_col = (t_col % blk) >= half
        if blk < SUBLANES:
            mid = (t_idx // blk) * blk + (half - 1)
            span = (j_idx > jnp.minimum(t_idx, mid)) & (j_idx <= jnp.maximum(t_idx, mid))
            expo = _dot_split(jnp.where(span, 1.0, 0.0).astype(BF16), parts[:2])
        else:
            by_block = cum.reshape(c // blk, blk, dk)
            cum_mid = jnp.broadcast_to(by_block[:, half - 1:half, :], by_block.shape).reshape(c, dk)
            expo = jnp.where(later_col, cum - cum_mid, cum_mid - cum)
        mixed = (jnp.where(later_col, q, kk) * jnp.exp2(expo)).astype(BF16)
        wanted = ((t_idx // blk) == (j_idx // blk)) & ((t_idx % blk) >= half) & ((j_idx % blk) < half)
        for hd in range(HGRN_HEADS):
            gram = _dot_nt(head(mixed, hd), head(mixed, hd))
            pair[hd] = jnp.where(wanted, gram, 0.0 if pair[hd] is None else pair[hd])
        blk *= 2

    outs = []
    for hd in range(HGRN_HEADS):
        st = state[hd]
        v_h = head(v, hd)
        o_h = _dot_nt(head(q_in, hd), st.astype(BF16)) + _dot(pair[hd].astype(BF16), v_h)
        o_h = o_h + jnp.sum(head(q, hd) * head(kk, hd), axis=-1, keepdims=True) * head(v_f32, hd)
        state[hd] = head(decay, hd) * st + _dot_tn(v_h, head(k_out, hd))
        outs.append(o_h * lax.rsqrt(jnp.mean(o_h * o_h, axis=-1, keepdims=True) + RMS_EPS))
    o = jnp.concatenate(outs, axis=-1) * ng_ref[...] * _silu(g_out)
    o_ref[0] = x + _dot(o.astype(BF16), wout_ref[...])


def _hgrn_mixer(x, g, w_in, lb_logits, norm_g, w_out, *, layer, c=256):
    b, s, d = x.shape
    tile = pl.BlockSpec((1, c, d), lambda bi, si: (bi, si, 0))
    return pl.pallas_call(
        functools.partial(_hgrn_kernel, c=c, layer=layer),
        grid=(b, s // c),
        in_specs=[tile, _resident((1, d)), _resident(w_in.shape), _resident(lb_logits.shape),
                  _resident((1, d)), _resident((d, d))],
        out_specs=tile,
        out_shape=jax.ShapeDtypeStruct((b, s, d), F32),
        scratch_shapes=[pltpu.VMEM((HGRN_HEADS, HGRN_HEAD_DIM, HGRN_HEAD_DIM), F32)],
        compiler_params=_cparams(2),
        name="hgrn_mixer",
    )(x, g.reshape(1, d), w_in.astype(BF16), lb_logits, norm_g.reshape(1, d), w_out.astype(BF16))


def kernel(x, ffn_norm, ffn_w_gate, ffn_w_up, ffn_w_down, mix_norm, final_norm, conv_w_in, conv_b_in, conv_dw, conv_dw_b, conv_ln_g, conv_ln_b, conv_w_out, fox_w_in, fox_b_f, fox_w_out, hgrn_w_in, hgrn_lb_logits, hgrn_norm, hgrn_w_out, pool_w, pool_scale):
    b, s, d = x.shape
    depth = ffn_norm.shape[0]
    n_mixers = 4
    wg = ffn_w_gate.astype(BF16)
    wu = ffn_w_up.astype(BF16)
    wd = ffn_w_down.astype(BF16)

    def ffn(xx, i, k, final_g=None):
        return _ffn(xx.reshape(b * s, d), ffn_norm[i, k], wg[i, k], wu[i, k], wd[i, k], final_g).reshape(b, s, d)

    for i in range(depth):
        m, j = i % n_mixers, i // n_mixers
        x = ffn(x, i, 0)
        if m == 0:
            x = _conv_mixer(x, mix_norm[i], conv_w_in[j], conv_b_in[j], conv_dw[j], conv_dw_b[j],
                            conv_ln_g[j], conv_ln_b[j], conv_w_out[j])
        elif m == 1:
            x = _fox_mixer(x, mix_norm[i], fox_w_in[j], fox_b_f[j], fox_w_out[j])
        elif m == 2:
            x = _hgrn_mixer(x, mix_norm[i], hgrn_w_in[j], hgrn_lb_logits, hgrn_norm[j], hgrn_w_out[j], layer=i)
        else:
            x = _pool_mixer(x, mix_norm[i], pool_w[j], pool_scale[j])
        x = ffn(x, i, 1, final_norm if i == depth - 1 else None)
    return x
```

```python
import functools

import jax
import jax.numpy as jnp
import numpy as np
from jax import lax
from jax.experimental import pallas as pl
from jax.experimental.pallas import tpu as pltpu

F32 = jnp.float32
BF16 = jnp.bfloat16

RMS_EPS = 1e-6
LN_EPS = 1e-5
LOG2_E = 1.4426950408889634
CONV_WIDTH = 31
CONV_HALO = 32
CONV_ROWS = 128
FOX_HEADS = 16
FOX_HEAD_DIM = 64
FOX_BIAS_TERMS = 3
FOX_VT_ROWS = 80
HGRN_HEADS = 8
HGRN_HEAD_DIM = 128
POOL_WINDOWS = (2, 4, 8, 16)
POOL_HALO = 16
LANES = 128
SUBLANES = 8
VMEM_LIMIT_BYTES = 56 * 1024 * 1024


def _cparams(n_grid_dims):
    return pltpu.CompilerParams(
        dimension_semantics=("arbitrary",) * n_grid_dims,
        vmem_limit_bytes=VMEM_LIMIT_BYTES)


def _resident(shape):
    zeros = (0,) * len(shape)
    return pl.BlockSpec(shape, lambda *_: zeros, pipeline_mode=pl.Buffered(1))


def _rms(x, g):
    return x * lax.rsqrt(jnp.mean(x * x, axis=-1, keepdims=True) + RMS_EPS) * g


def _silu(x):
    return x * jax.nn.sigmoid(x)


def _log_sigmoid(x):
    return jnp.minimum(x, 0.0) - jnp.log1p(jnp.exp(-jnp.abs(x)))


def _dot(a, b):
    return jnp.dot(a, b, preferred_element_type=F32)


def _dot_nt(a, b):
    return lax.dot_general(a, b, (((1,), (1,)), ((), ())), preferred_element_type=F32)


def _dot_tn(a, b):
    return lax.dot_general(a, b, (((0,), (0,)), ((), ())), preferred_element_type=F32)


def _split_bf16(x, parts):
    out = []
    for _ in range(parts - 1):
        p = x.astype(BF16)
        out.append(p)
        x = x - p.astype(F32)
    out.append(x.astype(BF16))
    return out


def _dot_split(w, parts):
    acc = _dot(w, parts[0])
    for p in parts[1:]:
        acc = acc + _dot(w, p)
    return acc


def _ffn_kernel(x_ref, g_ref, wg_ref, wu_ref, wd_ref, *rest, ff_chunks, final_norm):
    o_ref = rest[-1]
    x = x_ref[...]
    h = _rms(x, g_ref[...]).astype(BF16)
    acc = jnp.zeros_like(x)
    for c0, c1 in ff_chunks:
        gate = _dot(h, wg_ref[:, c0:c1])
        up = _dot(h, wu_ref[:, c0:c1])
        act = (_silu(gate) * up).astype(BF16)
        acc = acc + _dot(act, wd_ref[c0:c1, :])
    y = x + 0.5 * acc
    if final_norm:
        y = _rms(y, rest[0][...])
    o_ref[...] = y


def _ffn_chunks(d_ff, width):
    return tuple((c0, min(c0 + width, d_ff)) for c0 in range(0, d_ff, width))


def _ffn(x2, g, wg, wu, wd, final_g=None, *, tm=512, chunk=768):
    m, d = x2.shape
    d_ff = wg.shape[1]
    final_norm = final_g is not None
    row = pl.BlockSpec((tm, d), lambda i: (i, 0))
    in_specs = [row, _resident((1, d)), _resident((d, d_ff)), _resident((d, d_ff)), _resident((d_ff, d))]
    args = [x2, g.reshape(1, d), wg, wu, wd]
    if final_norm:
        in_specs.append(_resident((1, d)))
        args.append(final_g.reshape(1, d))
    return pl.pallas_call(
        functools.partial(_ffn_kernel, ff_chunks=_ffn_chunks(d_ff, chunk), final_norm=final_norm),
        grid=(m // tm,),
        in_specs=in_specs,
        out_specs=row,
        out_shape=jax.ShapeDtypeStruct((m, d), F32),
        compiler_params=_cparams(1),
        name="ffn",
    )(*args)


def _conv_kernel(x_ref, g_ref, win_ref, bin_ref, dw_ref, dwb_ref, lng_ref, lnb_ref, wout_ref, o_ref,
                 ubuf, cbuf, *, tm):
    d = x_ref.shape[-1]

    @pl.when(pl.program_id(1) == 0)
    def _():
        ubuf[0:CONV_HALO, :] = jnp.zeros((CONV_HALO, d), F32)

    x = x_ref[0]
    h = _rms(x, g_ref[...]).astype(BF16)
    p = _dot(h, win_ref[...]) + bin_ref[...]
    ubuf[CONV_HALO:CONV_HALO + tm, :] = p[:, :d] * jax.nn.sigmoid(p[:, d:])

    base = CONV_HALO - (CONV_WIDTH - 1)
    n_rows = CONV_ROWS + CONV_HALO
    for r0 in range(0, tm, CONV_ROWS):
        for c0 in range(0, d, LANES):
            cs = slice(c0, c0 + LANES)
            col = ubuf[r0:r0 + n_rows, cs]
            acc = jnp.broadcast_to(dwb_ref[:, cs], (CONV_ROWS, LANES))
            for r in range(SUBLANES):
                taps = [k for k in range(CONV_WIDTH) if (base + k) % SUBLANES == r]
                if not taps:
                    continue
                rot = col if r == 0 else pltpu.roll(col, n_rows - r, 0)
                for k in taps:
                    a0 = base + k - r
                    acc = acc + dw_ref[k:k + 1, cs] * rot[a0:a0 + CONV_ROWS]
            cbuf[r0:r0 + CONV_ROWS, cs] = acc
    ubuf[0:CONV_HALO, :] = ubuf[tm:tm + CONV_HALO, :]

    c = cbuf[...]
    xc = c - jnp.mean(c, axis=-1, keepdims=True)
    y = xc * lax.rsqrt(jnp.mean(xc * xc, axis=-1, keepdims=True) + LN_EPS) * lng_ref[...] + lnb_ref[...]
    o_ref[0] = x + _dot(_silu(y).astype(BF16), wout_ref[...])


def _conv_mixer(x, g, w_in, b_in, dw, dw_b, ln_g, ln_b, w_out, *, tm=256):
    b, s, d = x.shape
    dw_pad = jnp.zeros((CONV_HALO, d), F32).at[:CONV_WIDTH].set(dw)
    tile = pl.BlockSpec((1, tm, d), lambda bi, si: (bi, si, 0))
    return pl.pallas_call(
        functools.partial(_conv_kernel, tm=tm),
        grid=(b, s // tm),
        in_specs=[tile, _resident((1, d)), _resident((d, 2 * d)), _resident((1, 2 * d)),
                  _resident((CONV_HALO, d)), _resident((1, d)), _resident((1, d)), _resident((1, d)),
                  _resident((d, d))],
        out_specs=tile,
        out_shape=jax.ShapeDtypeStruct((b, s, d), F32),
        scratch_shapes=[pltpu.VMEM((CONV_HALO + tm, d), F32), pltpu.VMEM((tm, d), F32)],
        compiler_params=_cparams(2),
        name="conv_mixer",
    )(x, g.reshape(1, d), w_in.astype(BF16), b_in.reshape(1, 2 * d), dw_pad, dw_b.reshape(1, d),
      ln_g.reshape(1, d), ln_b.reshape(1, d), w_out.astype(BF16))


def _pool_kernel(x_ref, g_ref, w_ref, sc_ref, o_ref, hbuf, *, tm):
    d = x_ref.shape[-1]
    group = d // len(POOL_WINDOWS)
    si = pl.program_id(1)

    @pl.when(si == 0)
    def _():
        hbuf[0:POOL_HALO, :] = jnp.zeros((POOL_HALO, d), F32)

    x = x_ref[0]
    hbuf[POOL_HALO:POOL_HALO + tm, :] = _rms(x, g_ref[...])
    pos = (si * tm + 1 + lax.broadcasted_iota(jnp.int32, (tm, 1), 0)).astype(F32)
    outs = []
    n_rows = POOL_HALO + tm
    for gi, win in enumerate(POOL_WINDOWS):
        cs = slice(gi * group, (gi + 1) * group)
        tot = hbuf[:, cs]
        span = 1
        while span < win:
            tot = tot + pltpu.roll(tot, span, 0)
            span *= 2
        cur = hbuf[POOL_HALO:n_rows, cs]
        diff = tot[POOL_HALO:n_rows] / jnp.minimum(pos, float(win)) - cur
        outs.append(_dot(diff.astype(BF16), w_ref[gi]))
    hbuf[0:POOL_HALO, :] = hbuf[tm:tm + POOL_HALO, :]
    o_ref[0] = x + jnp.concatenate(outs, axis=-1) * sc_ref[...]


def _pool_mixer(x, g, w, scale, *, tm=512):
    b, s, d = x.shape
    tile = pl.BlockSpec((1, tm, d), lambda bi, si: (bi, si, 0))
    return pl.pallas_call(
        functools.partial(_pool_kernel, tm=tm),
        grid=(b, s // tm),
        in_specs=[tile, _resident((1, d)), _resident(w.shape), _resident((1, d))],
        out_specs=tile,
        out_shape=jax.ShapeDtypeStruct((b, s, d), F32),
        scratch_shapes=[pltpu.VMEM((POOL_HALO + tm, d), F32)],
        compiler_params=_cparams(2),
        name="pool_mixer",
    )(x, g.reshape(1, d), w.astype(BF16), scale.reshape(1, d))


def _fox_bias_selector():
    sel = np.zeros((FOX_BIAS_TERMS * LANES, FOX_HEADS // 2 * LANES), np.float32)
    for hd in range(FOX_HEADS):
        z = hd // 2 * LANES + (FOX_HEAD_DIM if hd % 2 == 0 else 0)
        for part in range(FOX_BIAS_TERMS):
            sel[part * LANES + hd, z + part] = 1.0
            sel[part * LANES + hd, z + FOX_BIAS_TERMS + part] = 1.0
    return sel


def _fox_proj_kernel(x_ref, g_ref, wqk_ref, wvt_ref, wf_ref, bf_ref, sel_ref, q_ref, k_ref, vt_ref, carry, *, tm):
    d = x_ref.shape[-1]

    @pl.when(pl.program_id(1) == 0)
    def _():
        carry[...] = jnp.zeros_like(carry)

    h = _rms(x_ref[0], g_ref[...]).astype(BF16)
    p = _dot(h, wqk_ref[...])
    vt = _dot_nt(wvt_ref[...], h).astype(BF16)
    ones = jnp.ones((FOX_VT_ROWS - FOX_HEAD_DIM, tm), BF16)
    for hd in range(FOX_HEADS):
        vt_ref[0, hd * FOX_VT_ROWS:hd * FOX_VT_ROWS + FOX_HEAD_DIM, :] = vt[hd * FOX_HEAD_DIM:(hd + 1) * FOX_HEAD_DIM]
        vt_ref[0, hd * FOX_VT_ROWS + FOX_HEAD_DIM:(hd + 1) * FOX_VT_ROWS, :] = ones

    log_f = _log_sigmoid(_dot(h, wf_ref[...]) + bf_ref[...])
    row = lax.broadcasted_iota(jnp.int32, (tm, tm), 0)
    col = lax.broadcasted_iota(jnp.int32, (tm, tm), 1)
    lower = jnp.where(col <= row, 1.0, 0.0).astype(BF16)
    c = carry[0:1, :] + _dot_split(lower, _split_bf16(log_f, 3))
    carry[...] = jnp.broadcast_to(c[tm - 1:tm, :], carry.shape)
    terms = _split_bf16(c * LOG2_E, FOX_BIAS_TERMS)
    bias = _dot(jnp.concatenate(terms, axis=-1), sel_ref[...])

    lane = lax.broadcasted_iota(jnp.int32, (1, LANES), 1)
    scale = FOX_HEAD_DIM ** -0.5 * LOG2_E
    for j in range(d // LANES):
        pq = p[:, j * LANES:(j + 1) * LANES] * scale
        pk = p[:, d + j * LANES:d + (j + 1) * LANES]
        bias_j = bias[:, j * LANES:(j + 1) * LANES]
        for a in range(2):
            hd = 2 * j + a
            data = (lane < FOX_HEAD_DIM) if a == 0 else (lane >= FOX_HEAD_DIM)
            z = FOX_HEAD_DIM if a == 0 else 0
            first = (lane >= z) & (lane < z + FOX_BIAS_TERMS)
            second = (lane >= z + FOX_BIAS_TERMS) & (lane < z + 2 * FOX_BIAS_TERMS)
            aug_q = jnp.where(first, bias_j, jnp.where(second, -1.0, 0.0))
            aug_k = jnp.where(second, bias_j, jnp.where(first, 1.0, 0.0))
            q_ref[0, hd] = jnp.where(data, pq, aug_q).astype(BF16)
            k_ref[0, hd] = jnp.where(data, pk, aug_k).astype(BF16)


def _fox_flash_kernel(q_ref, k_ref, vt_ref, o_ref, m_ref, acc_ref, *, tq, tk):
    qi = pl.program_id(1)
    key_idx = lax.broadcasted_iota(jnp.int32, (tk, tq), 0)
    qry_idx = lax.broadcasted_iota(jnp.int32, (tk, tq), 1)
    rows = FOX_VT_ROWS
    ratio = tq // tk
    heads = range(FOX_HEADS)
    m_ref[...] = jnp.full(m_ref.shape, -jnp.inf, F32)
    acc_ref[...] = jnp.zeros(acc_ref.shape, F32)

    def tile(kj, _, *, diag):
        k_start = pl.multiple_of(kj * tk, tk)
        scores = [_dot_nt(k_ref[0, hd, pl.ds(k_start, tk), :], q_ref[0, hd]) for hd in heads]
        probs, alphas = [], []
        for hd, s in zip(heads, scores):
            if diag is not None:
                s = jnp.where(key_idx + diag * tk <= qry_idx, s, -jnp.inf)
            m = m_ref[hd, 0:1, :]
            m_new = jnp.maximum(m, jnp.max(s, axis=0, keepdims=True))
            m_ref[hd] = jnp.broadcast_to(m_new, (SUBLANES, tq))
            alphas.append(jnp.exp2(m - m_new))
            probs.append(jnp.exp2(s - m_new).astype(BF16))
        for hd, p, alpha in zip(heads, probs, alphas):
            v_t = vt_ref[0, hd * rows:(hd + 1) * rows, pl.ds(k_start, tk)]
            acc_ref[hd] = alpha * acc_ref[hd] + _dot(v_t, p)
        return 0

    lax.fori_loop(0, qi * ratio, functools.partial(tile, diag=None), 0)
    for r in range(ratio):
        tile(qi * ratio + r, 0, diag=r)
    for hd in heads:
        o_ref[0, hd * FOX_HEAD_DIM:(hd + 1) * FOX_HEAD_DIM, :] = (
            acc_ref[hd, 0:FOX_HEAD_DIM, :] / acc_ref[hd, FOX_HEAD_DIM:FOX_HEAD_DIM + 1, :]).astype(BF16)


def _fox_out_kernel(x_ref, ot_ref, w_ref, y_ref):
    y_ref[0] = x_ref[0] + _dot_tn(ot_ref[0], w_ref[...])


def _fox_mixer(x, g, w_in, b_f, w_out, *, tm=512, tq=512, tk=256):
    b, s, d = x.shape
    vt_rows = FOX_HEADS * FOX_VT_ROWS
    w_qk = w_in[:, :2 * d].astype(BF16)
    w_vt = w_in[:, 2 * d:3 * d].T.astype(BF16)
    w_f = jnp.zeros((d, LANES), F32).at[:, :FOX_HEADS].set(w_in[:, 3 * d:]).astype(BF16)
    b_f_row = jnp.zeros((1, LANES), F32).at[0, :FOX_HEADS].set(b_f)
    tile = pl.BlockSpec((1, tm, d), lambda bi, si: (bi, si, 0))
    heads_tile = pl.BlockSpec((1, FOX_HEADS, tm, LANES), lambda bi, si: (bi, 0, si, 0))
    qk_shape = jax.ShapeDtypeStruct((b, FOX_HEADS, s, LANES), BF16)
    q, k, vt = pl.pallas_call(
        functools.partial(_fox_proj_kernel, tm=tm),
        grid=(b, s // tm),
        in_specs=[tile, _resident((1, d)), _resident((d, 2 * d)), _resident((d, d)), _resident((d, LANES)),
                  _resident((1, LANES)), _resident((FOX_BIAS_TERMS * LANES, FOX_HEADS // 2 * LANES))],
        out_specs=[heads_tile, heads_tile, pl.BlockSpec((1, vt_rows, tm), lambda bi, si: (bi, 0, si))],
        out_shape=[qk_shape, qk_shape, jax.ShapeDtypeStruct((b, vt_rows, s), BF16)],
        scratch_shapes=[pltpu.VMEM((SUBLANES, LANES), F32)],
        compiler_params=_cparams(2),
        name="fox_proj",
    )(x, g.reshape(1, d), w_qk, w_vt, w_f, b_f_row, jnp.asarray(_fox_bias_selector(), BF16))

    ot = pl.pallas_call(
        functools.partial(_fox_flash_kernel, tq=tq, tk=tk),
        grid=(b, s // tq),
        in_specs=[pl.BlockSpec((1, FOX_HEADS, tq, LANES), lambda bi, qi: (bi, 0, qi, 0)),
                  pl.BlockSpec((1, FOX_HEADS, s, LANES), lambda bi, qi: (bi, 0, 0, 0),
                               pipeline_mode=pl.Buffered(1)),
                  pl.BlockSpec((1, vt_rows, s), lambda bi, qi: (bi, 0, 0), pipeline_mode=pl.Buffered(1))],
        out_specs=pl.BlockSpec((1, d, tq), lambda bi, qi: (bi, 0, qi)),
        out_shape=jax.ShapeDtypeStruct((b, d, s), BF16),
        scratch_shapes=[pltpu.VMEM((FOX_HEADS, SUBLANES, tq), F32), pltpu.VMEM((FOX_HEADS, FOX_VT_ROWS, tq), F32)],
        compiler_params=_cparams(2),
        name="fox_flash",
    )(q, k, vt)

    return pl.pallas_call(
        _fox_out_kernel,
        grid=(b, s // tm),
        in_specs=[tile, pl.BlockSpec((1, d, tm), lambda bi, si: (bi, 0, si)), _resident((d, d))],
        out_specs=tile,
        out_shape=jax.ShapeDtypeStruct((b, s, d), F32),
        compiler_params=_cparams(2),
        name="fox_out",
    )(x, ot, w_out.astype(BF16))


def _hgrn_kernel(x_ref, g_ref, win_ref, lbl_ref, ng_ref, wout_ref, o_ref, state, *, c, layer):
    d = x_ref.shape[-1]
    dk = HGRN_HEADS * HGRN_HEAD_DIM

    @pl.when(pl.program_id(1) == 0)
    def _():
        state[...] = jnp.zeros_like(state)

    x = x_ref[0]
    h = _rms(x, g_ref[...]).astype(BF16)
    proj = _dot(h, win_ref[...])
    q = _silu(proj[:, :dk])
    f_raw = proj[:, dk:2 * dk]
    v_f32 = proj[:, 2 * dk:2 * dk + d]
    v = v_f32.astype(BF16)
    g_out = proj[:, 2 * dk + d:]

    logits = lbl_ref[...]
    e = jnp.exp(logits - jnp.max(logits, axis=0, keepdims=True))
    p_layers = e / jnp.sum(e, axis=0, keepdims=True)
    if layer > 0:
        lb = jnp.sum(p_layers[1:layer + 1], axis=0, keepdims=True)
    else:
        lb = jnp.zeros_like(logits[0:1])

    a_ = jnp.log(lb)
    b_ = jnp.log1p(-lb) + _log_sigmoid(f_raw)
    log_f = (jnp.maximum(a_, b_) + jnp.log1p(jnp.exp(-jnp.abs(a_ - b_)))) * LOG2_E
    kk = (1.0 - lb) * jax.nn.sigmoid(-f_raw)

    t_idx = lax.broadcasted_iota(jnp.int32, (c, c), 0)
    j_idx = lax.broadcasted_iota(jnp.int32, (c, c), 1)
    t_col = lax.broadcasted_iota(jnp.int32, (c, 1), 0)
    lower = jnp.where(j_idx <= t_idx, 1.0, 0.0).astype(BF16)
    cum = _dot_split(lower, _split_bf16(log_f, 3))
    cum_last = cum[c - 1:c, :]
    q_in = (q * jnp.exp2(cum)).astype(BF16)
    k_out = (kk * jnp.exp2(cum_last - cum)).astype(BF16)
    decay = jnp.exp2(cum_last)

    def head(arr, hd):
        return arr[:, hd * HGRN_HEAD_DIM:(hd + 1) * HGRN_HEAD_DIM]

    pair = [None] * HGRN_HEADS
    blk = 2
    while blk <= c:
        half = blk // 2
        later_col = (t_col % blk) >= half
        if blk == 2:
            expo = jnp.where(later_col, log_f, 0.0)
        elif blk == 4:
            place = t_col % blk
            expo = jnp.where(place == 2, log_f, jnp.where(place == 3, log_f + pltpu.roll(log_f, 1, 0),
                                                          jnp.where(place == 0, pltpu.roll(log_f, c - 1, 0), 0.0)))
        else:
            by_block = cum.reshape(c // blk, blk, dk)
            cum_mid = jnp.broadcast_to(by_block[:, half - 1:half, :], by_block.shape).reshape(c, dk)
            expo = jnp.where(later_col, cum - cum_mid, cum_mid - cum)
        mixed = (jnp.where(later_col, q, kk) * jnp.exp2(expo)).astype(BF16)
        wanted = ((t_idx // blk) == (j_idx // blk)) & ((t_idx % blk) >= half) & ((j_idx % blk) < half)
        for hd in range(HGRN_HEADS):
            gram = _dot_nt(head(mixed, hd), head(mixed, hd))
            pair[hd] = jnp.where(wanted, gram, 0.0 if pair[hd] is None else pair[hd])
        blk *= 2

    outs = []
    for hd in range(HGRN_HEADS):
        st = state[hd]
        v_h = head(v, hd)
        o_h = _dot_nt(head(q_in, hd), st.astype(BF16)) + _dot(pair[hd].astype(BF16), v_h)
        o_h = o_h + jnp.sum(head(q, hd) * head(kk, hd), axis=-1, keepdims=True) * head(v_f32, hd)
        state[hd] = head(decay, hd) * st + _dot_tn(v_h, head(k_out, hd))
        outs.append(o_h * lax.rsqrt(jnp.mean(o_h * o_h, axis=-1, keepdims=True) + RMS_EPS))
    o = jnp.concatenate(outs, axis=-1) * ng_ref[...] * _silu(g_out)
    o_ref[0] = x + _dot(o.astype(BF16), wout_ref[...])


def _hgrn_mixer(x, g, w_in, lb_logits, norm_g, w_out, *, layer, c=256):
    b, s, d = x.shape
    tile = pl.BlockSpec((1, c, d), lambda bi, si: (bi, si, 0))
    return pl.pallas_call(
        functools.partial(_hgrn_kernel, c=c, layer=layer),
        grid=(b, s // c),
        in_specs=[tile, _resident((1, d)), _resident(w_in.shape), _resident(lb_logits.shape),
                  _resident((1, d)), _resident((d, d))],
        out_specs=tile,
        out_shape=jax.ShapeDtypeStruct((b, s, d), F32),
        scratch_shapes=[pltpu.VMEM((HGRN_HEADS, HGRN_HEAD_DIM, HGRN_HEAD_DIM), F32)],
        compiler_params=_cparams(2),
        name="hgrn_mixer",
    )(x, g.reshape(1, d), w_in.astype(BF16), lb_logits, norm_g.reshape(1, d), w_out.astype(BF16))


def kernel(x, ffn_norm, ffn_w_gate, ffn_w_up, ffn_w_down, mix_norm, final_norm, conv_w_in, conv_b_in, conv_dw, conv_dw_b, conv_ln_g, conv_ln_b, conv_w_out, fox_w_in, fox_b_f, fox_w_out, hgrn_w_in, hgrn_lb_logits, hgrn_norm, hgrn_w_out, pool_w, pool_scale):
    b, s, d = x.shape
    depth = ffn_norm.shape[0]
    n_mixers = 4
    wg = ffn_w_gate.astype(BF16)
    wu = ffn_w_up.astype(BF16)
    wd = ffn_w_down.astype(BF16)

    def ffn(xx, i, k, final_g=None):
        return _ffn(xx.reshape(b * s, d), ffn_norm[i, k], wg[i, k], wu[i, k], wd[i, k], final_g).reshape(b, s, d)

    for i in range(depth):
        m, j = i % n_mixers, i // n_mixers
        x = ffn(x, i, 0)
        if m == 0:
            x = _conv_mixer(x, mix_norm[i], conv_w_in[j], conv_b_in[j], conv_dw[j], conv_dw_b[j],
                            conv_ln_g[j], conv_ln_b[j], conv_w_out[j])
        elif m == 1:
            x = _fox_mixer(x, mix_norm[i], fox_w_in[j], fox_b_f[j], fox_w_out[j])
        elif m == 2:
            x = _hgrn_mixer(x, mix_norm[i], hgrn_w_in[j], hgrn_lb_logits, hgrn_norm[j], hgrn_w_out[j], layer=i)
        else:
            x = _pool_mixer(x, mix_norm[i], pool_w[j], pool_scale[j])
        x = ffn(x, i, 1, final_norm if i == depth - 1 else None)
    return x
```

```python
import functools

import jax
import jax.numpy as jnp
import numpy as np
from jax import lax
from jax.experimental import pallas as pl
from jax.experimental.pallas import tpu as pltpu

F32 = jnp.float32
BF16 = jnp.bfloat16

RMS_EPS = 1e-6
LN_EPS = 1e-5
LOG2_E = 1.4426950408889634
CONV_WIDTH = 31
CONV_HALO = 32
CONV_ROWS = 128
FOX_HEADS = 16
FOX_HEAD_DIM = 64
FOX_BIAS_TERMS = 3
FOX_VT_ROWS = 80
HGRN_HEADS = 8
HGRN_HEAD_DIM = 128
POOL_WINDOWS = (2, 4, 8, 16)
POOL_HALO = 16
LANES = 128
SUBLANES = 8
VMEM_LIMIT_BYTES = 56 * 1024 * 1024


def _cparams(n_grid_dims):
    return pltpu.CompilerParams(
        dimension_semantics=("arbitrary",) * n_grid_dims,
        vmem_limit_bytes=VMEM_LIMIT_BYTES)


def _resident(shape):
    zeros = (0,) * len(shape)
    return pl.BlockSpec(shape, lambda *_: zeros, pipeline_mode=pl.Buffered(1))


def _rms(x, g):
    return x * lax.rsqrt(jnp.mean(x * x, axis=-1, keepdims=True) + RMS_EPS) * g


def _silu(x):
    return x * jax.nn.sigmoid(x)


def _log_sigmoid(x):
    return jnp.minimum(x, 0.0) - jnp.log1p(jnp.exp(-jnp.abs(x)))


def _dot(a, b):
    return jnp.dot(a, b, preferred_element_type=F32)


def _dot_nt(a, b):
    return lax.dot_general(a, b, (((1,), (1,)), ((), ())), preferred_element_type=F32)


def _dot_tn(a, b):
    return lax.dot_general(a, b, (((0,), (0,)), ((), ())), preferred_element_type=F32)


def _split_bf16(x, parts):
    out = []
    for _ in range(parts - 1):
        p = x.astype(BF16)
        out.append(p)
        x = x - p.astype(F32)
    out.append(x.astype(BF16))
    return out


def _dot_split(w, parts):
    acc = _dot(w, parts[0])
    for p in parts[1:]:
        acc = acc + _dot(w, p)
    return acc


def _ffn_kernel(x_ref, g_ref, wg_ref, wu_ref, wd_ref, *rest, ff_chunks, final_norm, attn_out):
    o_ref = rest[-1]
    x = x_ref[...]
    if attn_out:
        x = x + _dot_tn(rest[0][0], rest[1][...])
    h = _rms(x, g_ref[...]).astype(BF16)
    acc = jnp.zeros_like(x)
    for c0, c1 in ff_chunks:
        gate = _dot(h, wg_ref[:, c0:c1])
        up = _dot(h, wu_ref[:, c0:c1])
        act = (_silu(gate) * up).astype(BF16)
        acc = acc + _dot(act, wd_ref[c0:c1, :])
    y = x + 0.5 * acc
    if final_norm:
        y = _rms(y, rest[-2][...])
    o_ref[...] = y


def _ffn_chunks(d_ff, width):
    return tuple((c0, min(c0 + width, d_ff)) for c0 in range(0, d_ff, width))


def _ffn(x2, g, wg, wu, wd, final_g=None, attn=None, *, tm=512, chunk=768):
    m, d = x2.shape
    d_ff = wg.shape[1]
    final_norm = final_g is not None
    row = pl.BlockSpec((tm, d), lambda i: (i, 0))
    in_specs = [row, _resident((1, d)), _resident((d, d_ff)), _resident((d, d_ff)), _resident((d_ff, d))]
    args = [x2, g.reshape(1, d), wg, wu, wd]
    if attn is not None:
        ot, w_out = attn
        tiles_per_seq = ot.shape[2] // tm
        in_specs += [pl.BlockSpec((1, d, tm), lambda i: (i // tiles_per_seq, 0, i % tiles_per_seq)),
                     _resident((d, d))]
        args += [ot, w_out]
    if final_norm:
        in_specs.append(_resident((1, d)))
        args.append(final_g.reshape(1, d))
    return pl.pallas_call(
        functools.partial(_ffn_kernel, ff_chunks=_ffn_chunks(d_ff, chunk), final_norm=final_norm,
                          attn_out=attn is not None),
        grid=(m // tm,),
        in_specs=in_specs,
        out_specs=row,
        out_shape=jax.ShapeDtypeStruct((m, d), F32),
        compiler_params=_cparams(1),
        name="ffn",
    )(*args)


def _conv_kernel(x_ref, g_ref, win_ref, bin_ref, dw_ref, dwb_ref, lng_ref, lnb_ref, wout_ref, o_ref,
                 ubuf, cbuf, *, tm):
    d = x_ref.shape[-1]

    @pl.when(pl.program_id(1) == 0)
    def _():
        ubuf[0:CONV_HALO, :] = jnp.zeros((CONV_HALO, d), F32)

    x = x_ref[0]
    h = _rms(x, g_ref[...]).astype(BF16)
    p = _dot(h, win_ref[...]) + bin_ref[...]
    ubuf[CONV_HALO:CONV_HALO + tm, :] = p[:, :d] * jax.nn.sigmoid(p[:, d:])

    base = CONV_HALO - (CONV_WIDTH - 1)
    n_rows = CONV_ROWS + CONV_HALO
    for r0 in range(0, tm, CONV_ROWS):
        for c0 in range(0, d, LANES):
            cs = slice(c0, c0 + LANES)
            col = ubuf[r0:r0 + n_rows, cs]
            acc = jnp.broadcast_to(dwb_ref[:, cs], (CONV_ROWS, LANES))
            for r in range(SUBLANES):
                taps = [k for k in range(CONV_WIDTH) if (base + k) % SUBLANES == r]
                if not taps:
                    continue
                rot = col if r == 0 else pltpu.roll(col, n_rows - r, 0)
                for k in taps:
                    a0 = base + k - r
                    acc = acc + dw_ref[k:k + 1, cs] * rot[a0:a0 + CONV_ROWS]
            cbuf[r0:r0 + CONV_ROWS, cs] = acc
    ubuf[0:CONV_HALO, :] = ubuf[tm:tm + CONV_HALO, :]

    c = cbuf[...]
    xc = c - jnp.mean(c, axis=-1, keepdims=True)
    y = xc * lax.rsqrt(jnp.mean(xc * xc, axis=-1, keepdims=True) + LN_EPS) * lng_ref[...] + lnb_ref[...]
    o_ref[0] = x + _dot(_silu(y).astype(BF16), wout_ref[...])


def _conv_mixer(x, g, w_in, b_in, dw, dw_b, ln_g, ln_b, w_out, *, tm=256):
    b, s, d = x.shape
    dw_pad = jnp.zeros((CONV_HALO, d), F32).at[:CONV_WIDTH].set(dw)
    tile = pl.BlockSpec((1, tm, d), lambda bi, si: (bi, si, 0))
    return pl.pallas_call(
        functools.partial(_conv_kernel, tm=tm),
        grid=(b, s // tm),
        in_specs=[tile, _resident((1, d)), _resident((d, 2 * d)), _resident((1, 2 * d)),
                  _resident((CONV_HALO, d)), _resident((1, d)), _resident((1, d)), _resident((1, d)),
                  _resident((d, d))],
        out_specs=tile,
        out_shape=jax.ShapeDtypeStruct((b, s, d), F32),
        scratch_shapes=[pltpu.VMEM((CONV_HALO + tm, d), F32), pltpu.VMEM((tm, d), F32)],
        compiler_params=_cparams(2),
        name="conv_mixer",
    )(x, g.reshape(1, d), w_in.astype(BF16), b_in.reshape(1, 2 * d), dw_pad, dw_b.reshape(1, d),
      ln_g.reshape(1, d), ln_b.reshape(1, d), w_out.astype(BF16))


def _pool_kernel(x_ref, g_ref, w_ref, sc_ref, o_ref, hbuf, *, tm):
    d = x_ref.shape[-1]
    group = d // len(POOL_WINDOWS)
    si = pl.program_id(1)

    @pl.when(si == 0)
    def _():
        hbuf[0:POOL_HALO, :] = jnp.zeros((POOL_HALO, d), F32)

    x = x_ref[0]
    hbuf[POOL_HALO:POOL_HALO + tm, :] = _rms(x, g_ref[...])
    pos = (si * tm + 1 + lax.broadcasted_iota(jnp.int32, (tm, 1), 0)).astype(F32)
    outs = []
    n_rows = POOL_HALO + tm
    for gi, win in enumerate(POOL_WINDOWS):
        cs = slice(gi * group, (gi + 1) * group)
        tot = hbuf[:, cs]
        span = 1
        while span < win:
            tot = tot + pltpu.roll(tot, span, 0)
            span *= 2
        cur = hbuf[POOL_HALO:n_rows, cs]
        diff = tot[POOL_HALO:n_rows] / jnp.minimum(pos, float(win)) - cur
        outs.append(_dot(diff.astype(BF16), w_ref[gi]))
    hbuf[0:POOL_HALO, :] = hbuf[tm:tm + POOL_HALO, :]
    o_ref[0] = x + jnp.concatenate(outs, axis=-1) * sc_ref[...]


def _pool_mixer(x, g, w, scale, *, tm=512):
    b, s, d = x.shape
    tile = pl.BlockSpec((1, tm, d), lambda bi, si: (bi, si, 0))
    return pl.pallas_call(
        functools.partial(_pool_kernel, tm=tm),
        grid=(b, s // tm),
        in_specs=[tile, _resident((1, d)), _resident(w.shape), _resident((1, d))],
        out_specs=tile,
        out_shape=jax.ShapeDtypeStruct((b, s, d), F32),
        scratch_shapes=[pltpu.VMEM((POOL_HALO + tm, d), F32)],
        compiler_params=_cparams(2),
        name="pool_mixer",
    )(x, g.reshape(1, d), w.astype(BF16), scale.reshape(1, d))


def _fox_bias_selector():
    sel = np.zeros((FOX_BIAS_TERMS * LANES, FOX_HEADS // 2 * LANES), np.float32)
    for hd in range(FOX_HEADS):
        z = hd // 2 * LANES + (FOX_HEAD_DIM if hd % 2 == 0 else 0)
        for part in range(FOX_BIAS_TERMS):
            sel[part * LANES + hd, z + part] = 1.0
            sel[part * LANES + hd, z + FOX_BIAS_TERMS + part] = 1.0
    return sel


def _fox_proj_kernel(x_ref, g_ref, wqk_ref, wvt_ref, wf_ref, bf_ref, sel_ref, q_ref, k_ref, vt_ref, carry, *, tm):
    d = x_ref.shape[-1]

    @pl.when(pl.program_id(1) == 0)
    def _():
        carry[...] = jnp.zeros_like(carry)

    h = _rms(x_ref[0], g_ref[...]).astype(BF16)
    p = _dot(h, wqk_ref[...])
    vt = _dot_nt(wvt_ref[...], h).astype(BF16)
    ones = jnp.ones((FOX_VT_ROWS - FOX_HEAD_DIM, tm), BF16)
    for hd in range(FOX_HEADS):
        vt_ref[0, hd * FOX_VT_ROWS:hd * FOX_VT_ROWS + FOX_HEAD_DIM, :] = vt[hd * FOX_HEAD_DIM:(hd + 1) * FOX_HEAD_DIM]
        vt_ref[0, hd * FOX_VT_ROWS + FOX_HEAD_DIM:(hd + 1) * FOX_VT_ROWS, :] = ones

    log_f = _log_sigmoid(_dot(h, wf_ref[...]) + bf_ref[...])
    row = lax.broadcasted_iota(jnp.int32, (tm, tm), 0)
    col = lax.broadcasted_iota(jnp.int32, (tm, tm), 1)
    lower = jnp.where(col <= row, 1.0, 0.0).astype(BF16)
    c = carry[0:1, :] + _dot_split(lower, _split_bf16(log_f, 3))
    carry[...] = jnp.broadcast_to(c[tm - 1:tm, :], carry.shape)
    terms = _split_bf16(c * LOG2_E, FOX_BIAS_TERMS)
    bias = _dot(jnp.concatenate(terms, axis=-1), sel_ref[...])

    lane = lax.broadcasted_iota(jnp.int32, (1, LANES), 1)
    scale = FOX_HEAD_DIM ** -0.5 * LOG2_E
    for j in range(d // LANES):
        pq = p[:, j * LANES:(j + 1) * LANES] * scale
        pk = p[:, d + j * LANES:d + (j + 1) * LANES]
        bias_j = bias[:, j * LANES:(j + 1) * LANES]
        for a in range(2):
            hd = 2 * j + a
            data = (lane < FOX_HEAD_DIM) if a == 0 else (lane >= FOX_HEAD_DIM)
            z = FOX_HEAD_DIM if a == 0 else 0
            first = (lane >= z) & (lane < z + FOX_BIAS_TERMS)
            second = (lane >= z + FOX_BIAS_TERMS) & (lane < z + 2 * FOX_BIAS_TERMS)
            aug_q = jnp.where(first, bias_j, jnp.where(second, -1.0, 0.0))
            aug_k = jnp.where(second, bias_j, jnp.where(first, 1.0, 0.0))
            q_ref[0, hd] = jnp.where(data, pq, aug_q).astype(BF16)
            k_ref[0, hd] = jnp.where(data, pk, aug_k).astype(BF16)


def _fox_flash_kernel(q_ref, k_ref, vt_ref, o_ref, m_ref, acc_ref, *, tq, tk):
    qi = pl.program_id(1)
    key_idx = lax.broadcasted_iota(jnp.int32, (tk, tq), 0)
    qry_idx = lax.broadcasted_iota(jnp.int32, (tk, tq), 1)
    rows = FOX_VT_ROWS
    ratio = tq // tk
    heads = range(FOX_HEADS)
    m_ref[...] = jnp.full(m_ref.shape, -jnp.inf, F32)
    acc_ref[...] = jnp.zeros(acc_ref.shape, F32)

    def tile(kj, _, *, diag):
        k_start = pl.multiple_of(kj * tk, tk)
        scores = [_dot_nt(k_ref[0, hd, pl.ds(k_start, tk), :], q_ref[0, hd]) for hd in heads]
        probs, alphas = [], []
        for hd, s in zip(heads, scores):
            if diag is not None:
                s = jnp.where(key_idx + diag * tk <= qry_idx, s, -jnp.inf)
            m = m_ref[hd, 0:1, :]
            m_new = jnp.maximum(m, jnp.max(s, axis=0, keepdims=True))
            m_ref[hd] = jnp.broadcast_to(m_new, (SUBLANES, tq))
            alphas.append(jnp.exp2(m - m_new))
            probs.append(jnp.exp2(s - m_new).astype(BF16))
        for hd, p, alpha in zip(heads, probs, alphas):
            v_t = vt_ref[0, hd * rows:(hd + 1) * rows, pl.ds(k_start, tk)]
            acc_ref[hd] = alpha * acc_ref[hd] + _dot(v_t, p)
        return 0

    lax.fori_loop(0, qi * ratio, functools.partial(tile, diag=None), 0)
    for r in range(ratio):
        tile(qi * ratio + r, 0, diag=r)
    for hd in heads:
        o_ref[0, hd * FOX_HEAD_DIM:(hd + 1) * FOX_HEAD_DIM, :] = (
            acc_ref[hd, 0:FOX_HEAD_DIM, :] / acc_ref[hd, FOX_HEAD_DIM:FOX_HEAD_DIM + 1, :]).astype(BF16)


def _fox_mixer(x, g, w_in, b_f, w_out, *, tm=512, tq=512, tk=256):
    b, s, d = x.shape
    vt_rows = FOX_HEADS * FOX_VT_ROWS
    w_qk = w_in[:, :2 * d].astype(BF16)
    w_vt = w_in[:, 2 * d:3 * d].T.astype(BF16)
    w_f = jnp.zeros((d, LANES), F32).at[:, :FOX_HEADS].set(w_in[:, 3 * d:]).astype(BF16)
    b_f_row = jnp.zeros((1, LANES), F32).at[0, :FOX_HEADS].set(b_f)
    tile = pl.BlockSpec((1, tm, d), lambda bi, si: (bi, si, 0))
    heads_tile = pl.BlockSpec((1, FOX_HEADS, tm, LANES), lambda bi, si: (bi, 0, si, 0))
    qk_shape = jax.ShapeDtypeStruct((b, FOX_HEADS, s, LANES), BF16)
    q, k, vt = pl.pallas_call(
        functools.partial(_fox_proj_kernel, tm=tm),
        grid=(b, s // tm),
        in_specs=[tile, _resident((1, d)), _resident((d, 2 * d)), _resident((d, d)), _resident((d, LANES)),
                  _resident((1, LANES)), _resident((FOX_BIAS_TERMS * LANES, FOX_HEADS // 2 * LANES))],
        out_specs=[heads_tile, heads_tile, pl.BlockSpec((1, vt_rows, tm), lambda bi, si: (bi, 0, si))],
        out_shape=[qk_shape, qk_shape, jax.ShapeDtypeStruct((b, vt_rows, s), BF16)],
        scratch_shapes=[pltpu.VMEM((SUBLANES, LANES), F32)],
        compiler_params=_cparams(2),
        name="fox_proj",
    )(x, g.reshape(1, d), w_qk, w_vt, w_f, b_f_row, jnp.asarray(_fox_bias_selector(), BF16))

    ot = pl.pallas_call(
        functools.partial(_fox_flash_kernel, tq=tq, tk=tk),
        grid=(b, s // tq),
        in_specs=[pl.BlockSpec((1, FOX_HEADS, tq, LANES), lambda bi, qi: (bi, 0, qi, 0)),
                  pl.BlockSpec((1, FOX_HEADS, s, LANES), lambda bi, qi: (bi, 0, 0, 0),
                               pipeline_mode=pl.Buffered(1)),
                  pl.BlockSpec((1, vt_rows, s), lambda bi, qi: (bi, 0, 0), pipeline_mode=pl.Buffered(1))],
        out_specs=pl.BlockSpec((1, d, tq), lambda bi, qi: (bi, 0, qi)),
        out_shape=jax.ShapeDtypeStruct((b, d, s), BF16),
        scratch_shapes=[pltpu.VMEM((FOX_HEADS, SUBLANES, tq), F32), pltpu.VMEM((FOX_HEADS, FOX_VT_ROWS, tq), F32)],
        compiler_params=_cparams(2),
        name="fox_flash",
    )(q, k, vt)
    return ot, w_out.astype(BF16)


def _hgrn_kernel(x_ref, g_ref, win_ref, lbl_ref, ng_ref, wout_ref, o_ref, state, *, c, layer):
    d = x_ref.shape[-1]
    dk = HGRN_HEADS * HGRN_HEAD_DIM

    @pl.when(pl.program_id(1) == 0)
    def _():
        state[...] = jnp.zeros_like(state)

    x = x_ref[0]
    h = _rms(x, g_ref[...]).astype(BF16)
    proj = _dot(h, win_ref[...])
    q = _silu(proj[:, :dk])
    f_raw = proj[:, dk:2 * dk]
    v_f32 = proj[:, 2 * dk:2 * dk + d]
    v = v_f32.astype(BF16)
    g_out = proj[:, 2 * dk + d:]

    logits = lbl_ref[...]
    e = jnp.exp(logits - jnp.max(logits, axis=0, keepdims=True))
    p_layers = e / jnp.sum(e, axis=0, keepdims=True)
    if layer > 0:
        lb = jnp.sum(p_layers[1:layer + 1], axis=0, keepdims=True)
    else:
        lb = jnp.zeros_like(logits[0:1])

    a_ = jnp.log(lb)
    b_ = jnp.log1p(-lb) + _log_sigmoid(f_raw)
    log_f = (jnp.maximum(a_, b_) + jnp.log1p(jnp.exp(-jnp.abs(a_ - b_)))) * LOG2_E
    kk = (1.0 - lb) * jax.nn.sigmoid(-f_raw)

    t_idx = lax.broadcasted_iota(jnp.int32, (c, c), 0)
    j_idx = lax.broadcasted_iota(jnp.int32, (c, c), 1)
    t_col = lax.broadcasted_iota(jnp.int32, (c, 1), 0)
    lower = jnp.where(j_idx <= t_idx, 1.0, 0.0).astype(BF16)
    cum = _dot_split(lower, _split_bf16(log_f, 3))
    cum_last = cum[c - 1:c, :]
    q_in = (q * jnp.exp2(cum)).astype(BF16)
    k_out = (kk * jnp.exp2(cum_last - cum)).astype(BF16)
    decay = jnp.exp2(cum_last)

    def head(arr, hd):
        return arr[:, hd * HGRN_HEAD_DIM:(hd + 1) * HGRN_HEAD_DIM]

    pair = [None] * HGRN_HEADS
    blk = 2
    while blk <= c:
        half = blk // 2
        later_col = (t_col % blk) >= half
        if blk == 2:
            expo = jnp.where(later_col, log_f, 0.0)
        elif blk == 4:
            place = t_col % blk
            expo = jnp.where(place == 2, log_f, jnp.where(place == 3, log_f + pltpu.roll(log_f, 1, 0),
                                                          jnp.where(place == 0, pltpu.roll(log_f, c - 1, 0), 0.0)))
        else:
            by_block = cum.reshape(c // blk, blk, dk)
            cum_mid = jnp.broadcast_to(by_block[:, half - 1:half, :], by_block.shape).reshape(c, dk)
            expo = jnp.where(later_col, cum - cum_mid, cum_mid - cum)
        mixed = (jnp.where(later_col, q, kk) * jnp.exp2(expo)).astype(BF16)
        wanted = ((t_idx // blk) == (j_idx // blk)) & ((t_idx % blk) >= half) & ((j_idx % blk) < half)
        for hd in range(HGRN_HEADS):
            gram = _dot_nt(head(mixed, hd), head(mixed, hd))
            pair[hd] = jnp.where(wanted, gram, 0.0 if pair[hd] is None else pair[hd])
        blk *= 2

    outs = []
    for hd in range(HGRN_HEADS):
        st = state[hd]
        v_h = head(v, hd)
        o_h = _dot_nt(head(q_in, hd), st.astype(BF16)) + _dot(pair[hd].astype(BF16), v_h)
        o_h = o_h + jnp.sum(head(q, hd) * head(kk, hd), axis=-1, keepdims=True) * head(v_f32, hd)
        state[hd] = head(decay, hd) * st + _dot_tn(v_h, head(k_out, hd))
        outs.append(o_h * lax.rsqrt(jnp.mean(o_h * o_h, axis=-1, keepdims=True) + RMS_EPS))
    o = jnp.concatenate(outs, axis=-1) * ng_ref[...] * _silu(g_out)
    o_ref[0] = x + _dot(o.astype(BF16), wout_ref[...])


def _hgrn_mixer(x, g, w_in, lb_logits, norm_g, w_out, *, layer, c=256):
    b, s, d = x.shape
    tile = pl.BlockSpec((1, c, d), lambda bi, si: (bi, si, 0))
    return pl.pallas_call(
        functools.partial(_hgrn_kernel, c=c, layer=layer),
        grid=(b, s // c),
        in_specs=[tile, _resident((1, d)), _resident(w_in.shape), _resident(lb_logits.shape),
                  _resident((1, d)), _resident((d, d))],
        out_specs=tile,
        out_shape=jax.ShapeDtypeStruct((b, s, d), F32),
        scratch_shapes=[pltpu.VMEM((HGRN_HEADS, HGRN_HEAD_DIM, HGRN_HEAD_DIM), F32)],
        compiler_params=_cparams(2),
        name="hgrn_mixer",
    )(x, g.reshape(1, d), w_in.astype(BF16), lb_logits, norm_g.reshape(1, d), w_out.astype(BF16))


def kernel(x, ffn_norm, ffn_w_gate, ffn_w_up, ffn_w_down, mix_norm, final_norm, conv_w_in, conv_b_in, conv_dw, conv_dw_b, conv_ln_g, conv_ln_b, conv_w_out, fox_w_in, fox_b_f, fox_w_out, hgrn_w_in, hgrn_lb_logits, hgrn_norm, hgrn_w_out, pool_w, pool_scale):
    b, s, d = x.shape
    depth = ffn_norm.shape[0]
    n_mixers = 4
    wg = ffn_w_gate.astype(BF16)
    wu = ffn_w_up.astype(BF16)
    wd = ffn_w_down.astype(BF16)

    def ffn(xx, i, k, final_g=None, attn=None):
        return _ffn(xx.reshape(b * s, d), ffn_norm[i, k], wg[i, k], wu[i, k], wd[i, k], final_g, attn).reshape(b, s, d)

    for i in range(depth):
        m, j = i % n_mixers, i // n_mixers
        attn = None
        x = ffn(x, i, 0)
        if m == 0:
            x = _conv_mixer(x, mix_norm[i], conv_w_in[j], conv_b_in[j], conv_dw[j], conv_dw_b[j],
                            conv_ln_g[j], conv_ln_b[j], conv_w_out[j])
        elif m == 1:
            attn = _fox_mixer(x, mix_norm[i], fox_w_in[j], fox_b_f[j], fox_w_out[j])
        elif m == 2:
            x = _hgrn_mixer(x, mix_norm[i], hgrn_w_in[j], hgrn_lb_logits, hgrn_norm[j], hgrn_w_out[j], layer=i)
        else:
            x = _pool_mixer(x, mix_norm[i], pool_w[j], pool_scale[j])
        x = ffn(x, i, 1, final_norm if i == depth - 1 else None, attn)
    return x
```
